```python
import jax, jax.numpy as jnp
from jax import lax
import numpy as np

D_MODEL = 2048
BATCH = 4
SEQ = 2048
DEPTH = 4
DEC_BATCH = 128
DEC_SEQ = 1
PAST_LEN = 16384
PAGE_SIZE = 128

N_MIXERS = 3
N_HGRN_LAYERS = (DEPTH + 2) // 3
N_GLA_LAYERS = (DEPTH + 1) // 3
N_POOL_LAYERS = DEPTH // 3

HG_HEADS = 16
HG_DK = 128
HG_DV = D_MODEL // HG_HEADS
HG_F = HG_HEADS * HG_DK
HG_I = HG_HEADS * HG_DV

GLA_HEADS = 4
GLA_K = D_MODEL // 2
GLA_V = D_MODEL
GLA_DK = GLA_K // GLA_HEADS
GLA_DV = GLA_V // GLA_HEADS
GLA_GATE_RANK = 16
GLA_GATE_NORMALIZER = 16.0

POOL_WINDOWS = (2, 4, 8, 16)
POOL_GROUPS = len(POOL_WINDOWS)
POOL_GC = D_MODEL // POOL_GROUPS
POOL_BUF = max(POOL_WINDOWS) - 1

N_MEM = 256
XA_HEADS = 4
XA_DH = D_MODEL // XA_HEADS

D_FF = 4 * D_MODEL
CHUNK = 32
N_NORMS = 6
EPS = 1e-6

kernel_name = 'hybrid_hgrn2_gla_pool_memxattn_decode_step'


def rmsnorm(x, g):
    xf = x.astype(jnp.float32)
    y = xf * lax.rsqrt(jnp.mean(xf * xf, axis=-1, keepdims=True) + EPS)
    return (y * g.astype(jnp.float32)).astype(x.dtype)


def gated_head_rmsnorm(o, gate, g):
    o = o.astype(jnp.float32)
    y = o * lax.rsqrt(jnp.mean(o * o, axis=-1, keepdims=True) + EPS) * g.astype(jnp.float32)
    return y * jax.nn.silu(gate.astype(jnp.float32))


def chunk_gated_linear(q, k, v, logg, s0):
    B, L, H, _ = q.shape
    dv = v.shape[-1]
    c = min(CHUNK, L)
    n = -(-L // c)
    pad = n * c - L

    def to_chunks(t):
        t = jnp.pad(t, ((0, 0), (0, pad), (0, 0), (0, 0)))
        return t.reshape(B, n, c, H, t.shape[-1]).transpose(1, 0, 3, 2, 4)

    xs = tuple(to_chunks(t.astype(jnp.float32)) for t in (q, k, v, logg))
    causal = jnp.asarray(np.tril(np.ones((c, c), dtype=bool)))[:, :, None]

    def step(state, inp):
        qc, kc, vc, gc = inp
        b = jnp.cumsum(gc, axis=2)
        b_last = b[:, :, -1:, :]
        o_inter = jnp.einsum('bhtk,bhkv->bhtv', qc * jnp.exp(b), state)
        rel = b[:, :, :, None, :] - b[:, :, None, :, :]
        decay = jnp.where(causal, jnp.exp(jnp.where(causal, rel, 0.0)), 0.0)
        scores = jnp.einsum('bhtk,bhsk,bhtsk->bhts', qc, kc, decay)
        o = o_inter + jnp.einsum('bhts,bhsv->bhtv', scores, vc)
        state = (jnp.exp(b_last)[:, :, 0, :, None] * state
                 + jnp.einsum('bhsk,bhsv->bhkv', kc * jnp.exp(b_last - b), vc))
        return state, o

    state, o = lax.scan(step, s0.astype(jnp.float32), xs)
    o = o.transpose(1, 0, 3, 2, 4).reshape(B, n * c, H, dv)[:, :L]
    return o, state


def hgrn_lower_bounds(lb_logits):
    p = jax.nn.softmax(lb_logits.astype(jnp.float32), axis=0)
    return jnp.cumsum(p, axis=0) - p[0]


def hgrn2_mixer(u, s0, lb, w_in, g_norm, w_o):
    B, L, _ = u.shape
    proj = u @ w_in
    q, fz, i, g = jnp.split(proj, [HG_F, 2 * HG_F, 2 * HG_F + HG_I], axis=-1)
    q = jax.nn.silu(q.astype(jnp.float32)).reshape(B, L, HG_HEADS, HG_DK)
    fz = fz.astype(jnp.float32).reshape(B, L, HG_HEADS, HG_DK)
    lb = lb.astype(jnp.float32).reshape(HG_HEADS, HG_DK)
    ls = jax.nn.log_sigmoid(fz)
    pos = lb > 0
    lb_safe = jnp.where(pos, lb, 1.0)
    logf = jnp.where(pos, jnp.logaddexp(jnp.log(lb_safe), jnp.log1p(-lb) + ls), ls)
    k = (1.0 - lb) * jax.nn.sigmoid(-fz)
    v = i.reshape(B, L, HG_HEADS, HG_DV)
    o, s = chunk_gated_linear(q, k, v, logf, s0)
    o = gated_head_rmsnorm(o, g.reshape(B, L, HG_HEADS, HG_DV), g_norm)
    return o.reshape(B, L, HG_I).astype(u.dtype) @ w_o, s


def gla_mixer(u, s0, w_in, w_gk1, w_gk2, b_gk, g_norm, w_o):
    B, L, _ = u.shape
    proj = u @ w_in
    q, k, v, g = jnp.split(proj, [GLA_K, 2 * GLA_K, 2 * GLA_K + GLA_V], axis=-1)
    gk = (u @ w_gk1) @ w_gk2 + b_gk
    logg = (jax.nn.log_sigmoid(gk.astype(jnp.float32)) / GLA_GATE_NORMALIZER).reshape(B, L, GLA_HEADS, GLA_DK)
    q = q.astype(jnp.float32).reshape(B, L, GLA_HEADS, GLA_DK) * (GLA_DK ** -0.5)
    k = k.reshape(B, L, GLA_HEADS, GLA_DK)
    v = v.reshape(B, L, GLA_HEADS, GLA_DV)
    o, s = chunk_gated_linear(q, k, v, logg, s0)
    o = gated_head_rmsnorm(o, g.reshape(B, L, GLA_HEADS, GLA_DV), g_norm)
    return o.reshape(B, L, GLA_V).astype(u.dtype) @ w_o, s


def pool_mixer(u, buf, w, scale):
    B, L, D = u.shape
    P = buf.shape[1]
    full = jnp.concatenate([buf.astype(u.dtype), u], axis=1)
    T = P + L
    cs = jnp.concatenate([jnp.zeros((B, 1, D), jnp.float32),
                          jnp.cumsum(full.astype(jnp.float32), axis=1)], axis=1)
    cs = cs.reshape(B, T + 1, POOL_GROUPS, POOL_GC)
    win = np.array(POOL_WINDOWS)
    end = np.arange(L) + P + 1
    start = np.maximum(end[:, None] - win[None, :], 0)
    cnt = jnp.asarray((end[:, None] - start).astype(np.float32))
    grp = np.arange(POOL_GROUPS)
    wsum = cs[:, end] - cs[:, start, grp]
    d = wsum / cnt[None, :, :, None] - u.astype(jnp.float32).reshape(B, L, POOL_GROUPS, POOL_GC)
    y = jnp.einsum('blgc,gcd->blgd', d.astype(u.dtype), w).reshape(B, L, D) * scale
    new_buf = full[:, max(T - POOL_BUF, 0):]
    return y, new_buf


def memory_kv(mem, gain, w_k, w_v):
    B = mem.shape[0]
    mn = rmsnorm(mem, gain)
    k = (mn @ w_k).reshape(B, N_MEM, XA_HEADS, XA_DH)
    v = (mn @ w_v).reshape(B, N_MEM, XA_HEADS, XA_DH)
    return k, v


def cross_attention(u, k, v, w_q, w_o):
    B, L, _ = u.shape
    q = (u @ w_q).reshape(B, L, XA_HEADS, XA_DH)
    s = jnp.einsum('blhd,bmhd->bhlm', q.astype(jnp.float32), k.astype(jnp.float32)) * (XA_DH ** -0.5)
    p = jax.nn.softmax(s, axis=-1)
    o = jnp.einsum('bhlm,bmhd->blhd', p, v.astype(jnp.float32))
    return o.reshape(B, L, D_MODEL).astype(u.dtype) @ w_o


def squared_relu_mlp(u, w_up, w_down):
    return jnp.square(jax.nn.relu(u @ w_up)) @ w_down


def trunk(x, s_hgrn, s_gla, buf_pool, mem_k, mem_v, norm_gains,
          hgrn_w_in, hgrn_lb, hgrn_g_norm, hgrn_w_o,
          gla_w_in, gla_w_gk1, gla_w_gk2, gla_b_gk, gla_g_norm, gla_w_o,
          pool_w, pool_scale, xa_w_q, xa_w_o, mlp_w_up, mlp_w_down):
    lbs = hgrn_lower_bounds(hgrn_lb)
    new_h, new_g, new_p = [], [], []
    for i in range(DEPTH):
        j = i // N_MIXERS
        kind = i % N_MIXERS
        u = rmsnorm(x, norm_gains[i, 0])
        if kind == 0:
            m, s = hgrn2_mixer(u, s_hgrn[j], lbs[i], hgrn_w_in[j], hgrn_g_norm[j], hgrn_w_o[j])
            new_h.append(s.astype(s_hgrn.dtype))
        elif kind == 1:
            m, s = gla_mixer(u, s_gla[j], gla_w_in[j], gla_w_gk1[j], gla_w_gk2[j], gla_b_gk[j],
                             gla_g_norm[j], gla_w_o[j])
            new_g.append(s.astype(s_gla.dtype))
        else:
            m, b = pool_mixer(u, buf_pool[j], pool_w[j], pool_scale[j])
            new_p.append(b.astype(buf_pool.dtype))
        x = x + rmsnorm(m, norm_gains[i, 1])
        u = rmsnorm(x, norm_gains[i, 2])
        x = x + rmsnorm(cross_attention(u, mem_k[i], mem_v[i], xa_w_q[i], xa_w_o[i]), norm_gains[i, 3])
        u = rmsnorm(x, norm_gains[i, 4])
        x = x + rmsnorm(squared_relu_mlp(u, mlp_w_up[i], mlp_w_down[i]), norm_gains[i, 5])
    return x, jnp.stack(new_h), jnp.stack(new_g), jnp.stack(new_p)


def setup_inputs(seed: int = 0) -> dict:
    key = jax.random.key(seed)
    ks = jax.random.split(key, 28)

    def nrm(i, shape, scale):
        return scale * jax.random.normal(ks[i], shape, jnp.float32)

    D = D_MODEL
    pool_rows = min(POOL_BUF, PAST_LEN)
    return {
        'x_prompt': nrm(0, (BATCH, SEQ, D), 1.0),
        'x_sample': nrm(1, (DEC_BATCH, DEC_SEQ, D), 1.0),
        'state_hgrn': nrm(2, (N_HGRN_LAYERS, DEC_BATCH, HG_HEADS, HG_DK, HG_DV), 0.5),
        'state_gla': nrm(3, (N_GLA_LAYERS, DEC_BATCH, GLA_HEADS, GLA_DK, GLA_DV), 0.5),
        'state_pool': nrm(4, (N_POOL_LAYERS, DEC_BATCH, pool_rows, D), 1.0),
        'cache_mem_k': nrm(5, (DEPTH, DEC_BATCH, N_MEM, XA_HEADS, XA_DH), 1.0),
        'cache_mem_v': nrm(6, (DEPTH, DEC_BATCH, N_MEM, XA_HEADS, XA_DH), 1.0),
        'mem_prompt': nrm(7, (BATCH, N_MEM, D), 1.0),
        'norm_gains': 1.0 + nrm(8, (DEPTH, N_NORMS, D), 0.1),
        'hgrn_w_in': nrm(9, (N_HGRN_LAYERS, D, 2 * HG_F + 2 * HG_I), D ** -0.5),
        'hgrn_lb': nrm(10, (DEPTH, HG_F), 1.0),
        'hgrn_g_norm': 1.0 + nrm(11, (N_HGRN_LAYERS, HG_DV), 0.1),
        'hgrn_w_o': nrm(12, (N_HGRN_LAYERS, HG_I, D), HG_I ** -0.5),
        'gla_w_in': nrm(13, (N_GLA_LAYERS, D, 2 * GLA_K + 2 * GLA_V), D ** -0.5),
        'gla_w_gk1': nrm(14, (N_GLA_LAYERS, D, GLA_GATE_RANK), D ** -0.5),
        'gla_w_gk2': nrm(15, (N_GLA_LAYERS, GLA_GATE_RANK, GLA_K), GLA_GATE_RANK ** -0.5),
        'gla_b_gk': nrm(16, (N_GLA_LAYERS, GLA_K), 0.1),
        'gla_g_norm': 1.0 + nrm(17, (N_GLA_LAYERS, GLA_DV), 0.1),
        'gla_w_o': nrm(18, (N_GLA_LAYERS, GLA_V, D), GLA_V ** -0.5),
        'pool_w': nrm(19, (N_POOL_LAYERS, POOL_GROUPS, POOL_GC, POOL_GC), POOL_GC ** -0.5),
        'pool_scale': 1.0 + nrm(20, (N_POOL_LAYERS, D), 0.1),
        'mem_norm': 1.0 + nrm(21, (DEPTH, D), 0.1),
        'xa_w_q': nrm(22, (DEPTH, D, D), D ** -0.5),
        'xa_w_k': nrm(23, (DEPTH, D, D), D ** -0.5),
        'xa_w_v': nrm(24, (DEPTH, D, D), D ** -0.5),
        'xa_w_o': nrm(25, (DEPTH, D, D), D ** -0.5),
        'mlp_w_up': nrm(26, (DEPTH, D, D_FF), D ** -0.5),
        'mlp_w_down': nrm(27, (DEPTH, D_FF, D), D_FF ** -0.5),
    }


def reference(x_prompt, x_sample, state_hgrn, state_gla, state_pool, cache_mem_k, cache_mem_v,
              mem_prompt, norm_gains, hgrn_w_in, hgrn_lb, hgrn_g_norm, hgrn_w_o,
              gla_w_in, gla_w_gk1, gla_w_gk2, gla_b_gk, gla_g_norm, gla_w_o,
              pool_w, pool_scale, mem_norm, xa_w_q, xa_w_k, xa_w_v, xa_w_o,
              mlp_w_up, mlp_w_down):
    weights = (norm_gains, hgrn_w_in, hgrn_lb, hgrn_g_norm, hgrn_w_o,
               gla_w_in, gla_w_gk1, gla_w_gk2, gla_b_gk, gla_g_norm, gla_w_o,
               pool_w, pool_scale, xa_w_q, xa_w_o, mlp_w_up, mlp_w_down)

    mk, mv = [], []
    for i in range(DEPTH):
        k, v = memory_kv(mem_prompt, mem_norm[i], xa_w_k[i], xa_w_v[i])
        mk.append(k)
        mv.append(v)
    mem_k_prompt = jnp.stack(mk)
    mem_v_prompt = jnp.stack(mv)

    Bp = x_prompt.shape[0]
    h0 = jnp.zeros((N_HGRN_LAYERS, Bp, HG_HEADS, HG_DK, HG_DV), state_hgrn.dtype)
    g0 = jnp.zeros((N_GLA_LAYERS, Bp, GLA_HEADS, GLA_DK, GLA_DV), state_gla.dtype)
    p0 = jnp.zeros((N_POOL_LAYERS, Bp, 0, D_MODEL), state_pool.dtype)
    y_prompt, h_p, g_p, p_p = trunk(x_prompt, h0, g0, p0, mem_k_prompt, mem_v_prompt, *weights)

    y_sample, h_s, g_s, p_s = trunk(x_sample, state_hgrn, state_gla, state_pool,
                                    cache_mem_k, cache_mem_v, *weights)
    return (y_prompt, y_sample, h_p, h_s, g_p, g_s, p_p, p_s, mem_k_prompt, mem_v_prompt)
```

```python
import functools

import jax
import jax.numpy as jnp
from jax import lax
from jax.experimental import pallas as pl
from jax.experimental.pallas import tpu as pltpu

f32 = jnp.float32
bf16 = jnp.bfloat16

EPS = 1e-6
POOL_WINDOWS = (2, 4, 8, 16)
GLA_GATE_NORMALIZER = 16.0
XA_HEADS = 4
HG_DK = 128
GLA_HEADS = 4
CHUNK = 32
SUBLANES = 8
VMEM_LIMIT = 56 * 1024 * 1024


def _params(*sem, vmem=VMEM_LIMIT):
    return pltpu.CompilerParams(dimension_semantics=sem, vmem_limit_bytes=vmem)


def _rms_scale(x):
    return lax.rsqrt(jnp.mean(x * x, axis=-1, keepdims=True) + EPS)


def _silu(x):
    return x / (1.0 + jnp.exp(-x))


def _rmsnorm_kernel(x_ref, g_ref, o_ref):
    x = x_ref[...]
    o_ref[...] = (x * _rms_scale(x) * g_ref[...]).astype(o_ref.dtype)


def _rmsnorm(x, g, out_dtype, tm):
    m, d = x.shape
    return pl.pallas_call(
        _rmsnorm_kernel,
        grid=(m // tm,),
        in_specs=[pl.BlockSpec((tm, d), lambda i: (i, 0)), pl.BlockSpec((1, d), lambda i: (0, 0))],
        out_specs=pl.BlockSpec((tm, d), lambda i: (i, 0)),
        out_shape=jax.ShapeDtypeStruct((m, d), out_dtype),
        compiler_params=_params("parallel"),
        name="rmsnorm",
    )(x, g)


def _add_norm_kernel(x_ref, m_ref, gp_ref, gn_ref, xo_ref, uo_ref):
    m = m_ref[...]
    xn = x_ref[...] + m * _rms_scale(m) * gp_ref[...]
    xo_ref[...] = xn
    uo_ref[...] = (xn * _rms_scale(xn) * gn_ref[...]).astype(uo_ref.dtype)


def _add_norm(x, m, g_post, g_next, tm):
    rows, d = x.shape
    row = pl.BlockSpec((tm, d), lambda i: (i, 0))
    gain = pl.BlockSpec((1, d), lambda i: (0, 0))
    return pl.pallas_call(
        _add_norm_kernel,
        grid=(rows // tm,),
        in_specs=[row, row, gain, gain],
        out_specs=[row, row],
        out_shape=[jax.ShapeDtypeStruct((rows, d), f32), jax.ShapeDtypeStruct((rows, d), bf16)],
        compiler_params=_params("parallel"),
        name="add_norm",
    )(x, m, g_post, g_next)


def _cast_kernel(x_ref, o_ref):
    o_ref[...] = x_ref[...].astype(o_ref.dtype)


def _cast_bf16(w, tr=512):
    nl, k, n = w.shape
    spec = pl.BlockSpec((None, tr, n), lambda l, i: (l, i, 0))
    return pl.pallas_call(
        _cast_kernel,
        grid=(nl, k // tr),
        in_specs=[spec],
        out_specs=spec,
        out_shape=jax.ShapeDtypeStruct(w.shape, bf16),
        compiler_params=_params("parallel", "parallel"),
        name="cast_bf16",
    )(w)


def _lower_bounds_kernel(l_ref, o_ref):
    z = l_ref[...]
    p = jnp.exp(z - jnp.max(z, axis=0, keepdims=True))
    p = p / jnp.sum(p, axis=0, keepdims=True)
    acc = jnp.zeros_like(p[0:1])
    for i in range(z.shape[0]):
        acc = acc + p[i:i + 1]
        o_ref[i:i + 1, :] = acc - p[0:1]


def _lower_bounds(lb_logits):
    return pl.pallas_call(
        _lower_bounds_kernel,
        out_shape=jax.ShapeDtypeStruct(lb_logits.shape, f32),
        name="hgrn_lower_bounds",
    )(lb_logits)


def _mm_ws_kernel(x_ref, w_ref, o_ref, wbf_ref, *, act):
    @pl.when(pl.program_id(1) == 0)
    def _():
        wbf_ref[...] = w_ref[...].astype(bf16)

    acc = jnp.dot(x_ref[...], wbf_ref[...], preferred_element_type=f32)
    if act == "relu2":
        acc = jnp.square(jnp.maximum(acc, 0.0))
    o_ref[...] = acc.astype(o_ref.dtype)


def _mm_ws(x, w, layer, *, tm, tn, out_dtype=f32, act=None):
    m, k = x.shape
    n = w.shape[2]
    return pl.pallas_call(
        functools.partial(_mm_ws_kernel, act=act),
        grid=(n // tn, m // tm),
        in_specs=[
            pl.BlockSpec((tm, k), lambda j, i: (i, 0)),
            pl.BlockSpec((None, k, tn), lambda j, i: (layer, 0, j)),
        ],
        out_specs=pl.BlockSpec((tm, tn), lambda j, i: (i, j)),
        out_shape=jax.ShapeDtypeStruct((m, n), out_dtype),
        scratch_shapes=[pltpu.VMEM((k, tn), bf16)],
        compiler_params=_params("arbitrary", "arbitrary"),
        name="matmul_ws",
    )(x, w)


def _mm_res_kernel(a_ref, w_ref, x_ref, gp_ref, *rest, nk, with_next):
    if with_next:
        gn_ref, xo_ref, uo_ref, acc_ref = rest
    else:
        xo_ref, acc_ref = rest
    kk = pl.program_id(1)
    part = jnp.dot(a_ref[...], w_ref[...], preferred_element_type=f32)

    @pl.when(kk == 0)
    def _():
        acc_ref[...] = part

    @pl.when(kk > 0)
    def _():
        acc_ref[...] += part

    @pl.when(kk == nk - 1)
    def _():
        m = acc_ref[...]
        xn = x_ref[...] + m * _rms_scale(m) * gp_ref[...]
        xo_ref[...] = xn
        if with_next:
            uo_ref[...] = (xn * _rms_scale(xn) * gn_ref[...]).astype(uo_ref.dtype)


def _mm_res(a, w, layer, x, g_post, g_next, *, tm, tk):
    m, k = a.shape
    d = w.shape[2]
    nk = k // tk
    with_next = g_next is not None
    row = pl.BlockSpec((tm, d), lambda i, kk: (i, 0))
    gain = pl.BlockSpec((1, d), lambda i, kk: (0, 0))
    in_specs = [
        pl.BlockSpec((tm, tk), lambda i, kk: (i, kk)),
        pl.BlockSpec((None, tk, d), lambda i, kk: (layer, kk, 0)),
        row,
        gain,
    ]
    args = [a, w, x, g_post]
    out_specs = [row]
    out_shape = [jax.ShapeDtypeStruct((m, d), f32)]
    if with_next:
        in_specs.append(gain)
        args.append(g_next)
        out_specs.append(row)
        out_shape.append(jax.ShapeDtypeStruct((m, d), bf16))
    res = pl.pallas_call(
        functools.partial(_mm_res_kernel, nk=nk, with_next=with_next),
        grid=(m // tm, nk),
        in_specs=in_specs,
        out_specs=out_specs,
        out_shape=out_shape,
        scratch_shapes=[pltpu.VMEM((tm, d), f32)],
        compiler_params=_params("parallel", "arbitrary"),
        name="matmul_residual_norm",
    )(*args)
    return (res[0], res[1]) if with_next else (res[0], None)


def _cumsum_rows(x):
    c = x.shape[0]
    row = lax.broadcasted_iota(jnp.int32, x.shape, 0)
    sh = 1
    while sh < c:
        x = x + jnp.where(row >= sh, pltpu.roll(x, sh, axis=0), 0.0)
        sh *= 2
    return x


def _col_bcast(r, width):
    n = r.shape[1]
    col = jnp.transpose(jnp.broadcast_to(r, (SUBLANES, n)))[:, 0:1]
    return jnp.broadcast_to(col, (n, width))


def _intra_scores(q, k, b):
    c = q.shape[0]
    groups = c // SUBLANES
    lane = lax.broadcasted_iota(jnp.int32, (SUBLANES, c), 1)
    acc = [jnp.zeros((SUBLANES, c), f32) for _ in range(groups)]
    for s in range(c):
        j = s // SUBLANES
        lo = SUBLANES * j
        w = jnp.exp(jnp.minimum(b[lo:, :] - b[s:s + 1, :], 0.0))
        col = jnp.sum(q[lo:, :] * w * k[s:s + 1, :], axis=-1, keepdims=True)
        for g in range(j, groups):
            r0 = SUBLANES * (g - j)
            acc[g] = jnp.where(lane == s, col[r0:r0 + SUBLANES, :], acc[g])
    a = jnp.concatenate(acc, axis=0)
    t_idx = lax.broadcasted_iota(jnp.int32, (c, c), 0)
    s_idx = lax.broadcasted_iota(jnp.int32, (c, c), 1)
    return jnp.where(t_idx >= s_idx, a, 0.0)


def _chunk_step(q, k, v, logg, s_ref):
    c = q.shape[0]
    dv = v.shape[1]
    b = _cumsum_rows(logg)
    b_last = b[c - 1:c, :]
    s_old = s_ref[...]
    vb = v.astype(bf16)
    o = jnp.dot((q * jnp.exp(b)).astype(bf16), s_old.astype(bf16), preferred_element_type=f32)
    a = _intra_scores(q, k, b)
    o = o + jnp.dot(a.astype(bf16), vb, preferred_element_type=f32)
    ke = (k * jnp.exp(b_last - b)).astype(bf16)
    kv = lax.dot_general(ke, vb, (((0,), (0,)), ((), ())), preferred_element_type=f32)
    s_ref[...] = _col_bcast(jnp.exp(b_last), dv) * s_old + kv
    return o


def _gated_head_norm(o, gate, gn):
    return o * _rms_scale(o) * gn * _silu(gate)


def _hgrn_features(q_raw, fz, lb):
    e = jnp.exp(-jnp.abs(fz))
    r = 1.0 / (1.0 + e)
    nonneg = fz >= 0
    sig = jnp.where(nonneg, r, e * r)
    sig_neg = jnp.where(nonneg, e * r, r)
    log_sig = jnp.minimum(fz, 0.0) - jnp.log1p(e)
    logf = jnp.where(lb > 0, jnp.log(lb + (1.0 - lb) * sig), log_sig)
    k = (1.0 - lb) * sig_neg
    return _silu(q_raw), k, logf


def _hgrn_prompt_kernel(q_ref, fz_ref, v_ref, g_ref, lb_ref, gn_ref, o_ref, so_ref, s_ref):
    t = pl.program_id(2)

    @pl.when(t == 0)
    def _():
        s_ref[...] = jnp.zeros_like(s_ref)

    lb = lb_ref[...]
    gn = gn_ref[...]

    def body(ci, carry):
        rows = pl.ds(pl.multiple_of(ci * CHUNK, CHUNK), CHUNK)
        q, k, logf = _hgrn_features(q_ref[rows, :], fz_ref[rows, :], lb)
        o = _chunk_step(q, k, v_ref[rows, :], logf, s_ref)
        o_ref[rows, :] = _gated_head_norm(o, g_ref[rows, :], gn).astype(o_ref.dtype)
        return carry

    lax.fori_loop(0, q_ref.shape[0] // CHUNK, body, 0)

    @pl.when(t == pl.num_programs(2) - 1)
    def _():
        so_ref[...] = s_ref[...]


def _hgrn_prompt(proj, lbs, layer, g_norm, j, *, batch, seq, heads, lt):
    dk = HG_DK
    dv = g_norm.shape[-1]
    nt = seq // lt
    col = lambda off: pl.BlockSpec((lt, dk), lambda b, h, t: (b * nt + t, off + h))
    return pl.pallas_call(
        _hgrn_prompt_kernel,
        grid=(batch, heads, nt),
        in_specs=[
            col(0), col(heads), col(2 * heads), col(3 * heads),
            pl.BlockSpec((None, 1, dk), lambda b, h, t: (layer, 0, h)),
            pl.BlockSpec((None, 1, dv), lambda b, h, t: (j, 0, 0)),
        ],
        out_specs=[
            pl.BlockSpec((lt, dv), lambda b, h, t: (b * nt + t, h)),
            pl.BlockSpec((None, None, dk, dv), lambda b, h, t: (b, h, 0, 0)),
        ],
        out_shape=[
            jax.ShapeDtypeStruct((batch * seq, heads * dv), bf16),
            jax.ShapeDtypeStruct((batch, heads, dk, dv), f32),
        ],
        scratch_shapes=[pltpu.VMEM((dk, dv), f32)],
        compiler_params=_params("parallel", "parallel", "arbitrary"),
        name="hgrn_prompt",
    )(proj, proj, proj, proj, lbs, g_norm)


def _gla_prompt_kernel(q_ref, k_ref, v_ref, g_ref, lg_ref, gn_ref, o_ref, so_ref, s_ref, *, q_scale):
    t = pl.program_id(2)

    @pl.when(t == 0)
    def _():
        s_ref[...] = jnp.zeros_like(s_ref)

    gn = gn_ref[...]

    def body(ci, carry):
        rows = pl.ds(pl.multiple_of(ci * CHUNK, CHUNK), CHUNK)
        o = _chunk_step(q_ref[rows, :] * q_scale, k_ref[rows, :], v_ref[rows, :], lg_ref[rows, :], s_ref)
        o_ref[rows, :] = _gated_head_norm(o, g_ref[rows, :], gn).astype(o_ref.dtype)
        return carry

    lax.fori_loop(0, q_ref.shape[0] // CHUNK, body, 0)

    @pl.when(t == pl.num_programs(2) - 1)
    def _():
        so_ref[...] = s_ref[...]


def _gla_prompt(proj, logg, g_norm, j, *, batch, seq, heads, dk, dv, lt):
    nt = seq // lt
    kcol = lambda off: pl.BlockSpec((lt, dk), lambda b, h, t: (b * nt + t, off + h))
    vcol = lambda off: pl.BlockSpec((lt, dv), lambda b, h, t: (b * nt + t, off + h))
    v_off = 2 * heads * dk // dv
    return pl.pallas_call(
        functools.partial(_gla_prompt_kernel, q_scale=dk ** -0.5),
        grid=(batch, heads, nt),
        in_specs=[
            kcol(0), kcol(heads), vcol(v_off), vcol(v_off + heads), kcol(0),
            pl.BlockSpec((None, 1, dv), lambda b, h, t: (j, 0, 0)),
        ],
        out_specs=[
            pl.BlockSpec((lt, dv), lambda b, h, t: (b * nt + t, h)),
            pl.BlockSpec((None, None, dk, dv), lambda b, h, t: (b, h, 0, 0)),
        ],
        out_shape=[
            jax.ShapeDtypeStruct((batch * seq, heads * dv), bf16),
            jax.ShapeDtypeStruct((batch, heads, dk, dv), f32),
        ],
        scratch_shapes=[pltpu.VMEM((dk, dv), f32)],
        compiler_params=_params("parallel", "parallel", "arbitrary"),
        name="gla_prompt",
    )(proj, proj, proj, proj, logg, g_norm)


def _state_step(q, k, v, logg, s_ref, so_ref, hh):
    bb, dk = q.shape
    dv = v.shape[1]
    cols = jnp.transpose(jnp.concatenate([jnp.exp(logg), k, q], axis=0))
    outs = []
    for r in range(bb):
        e_col = jnp.broadcast_to(cols[:, r:r + 1], (dk, dv))
        k_col = jnp.broadcast_to(cols[:, bb + r:bb + r + 1], (dk, dv))
        q_col = jnp.broadcast_to(cols[:, 2 * bb + r:2 * bb + r + 1], (dk, dv))
        s_new = e_col * s_ref[r, hh] + k_col * v[r:r + 1, :]
        so_ref[r, hh] = s_new
        outs.append(jnp.sum(q_col * s_new, axis=0, keepdims=True))
    return jnp.concatenate(outs, axis=0)


def _hgrn_sample_kernel(q_ref, fz_ref, v_ref, g_ref, lb_ref, gn_ref, s_ref, o_ref, so_ref, *, hb):
    dk = HG_DK
    dv = gn_ref.shape[1]
    gn = gn_ref[...]
    for hh in range(hb):
        kc = slice(hh * dk, (hh + 1) * dk)
        vc = slice(hh * dv, (hh + 1) * dv)
        q, k, logf = _hgrn_features(q_ref[:, kc], fz_ref[:, kc], lb_ref[:, kc])
        o = _state_step(q, k, v_ref[:, vc], logf, s_ref, so_ref, hh)
        o_ref[:, vc] = _gated_head_norm(o, g_ref[:, vc], gn)


def _hgrn_sample(proj, lbs, layer, g_norm, j, state, *, row0, batch, heads, bb, hb):
    dk = HG_DK
    dv = g_norm.shape[-1]
    rb0 = row0 // bb
    hblocks = heads // hb
    col = lambda off: pl.BlockSpec((bb, hb * dk), lambda b, h: (rb0 + b, off + h))
    st = pl.BlockSpec((None, bb, hb, dk, dv), lambda b, h: (j, b, h, 0, 0))
    return pl.pallas_call(
        functools.partial(_hgrn_sample_kernel, hb=hb),
        grid=(batch // bb, hblocks),
        in_specs=[
            col(0), col(hblocks), col(2 * hblocks), col(3 * hblocks),
            pl.BlockSpec((None, 1, hb * dk), lambda b, h: (layer, 0, h)),
            pl.BlockSpec((None, 1, dv), lambda b, h: (j, 0, 0)),
            st,
        ],
        out_specs=[
            pl.BlockSpec((bb, hb * dv), lambda b, h: (b, h)),
            pl.BlockSpec((bb, hb, dk, dv), lambda b, h: (b, h, 0, 0)),
        ],
        out_shape=[
            jax.ShapeDtypeStruct((batch, heads * dv), f32),
            jax.ShapeDtypeStruct((batch, heads, dk, dv), f32),
        ],
        compiler_params=_params("parallel", "parallel"),
        name="hgrn_sample",
    )(proj, proj, proj, proj, lbs, g_norm, state)


def _gla_sample_kernel(q_ref, k_ref, v_ref, g_ref, lg_ref, gn_ref, s_ref, o_ref, so_ref, *, q_scale):
    o = _state_step(q_ref[...] * q_scale, k_ref[...], v_ref[...], lg_ref[...], s_ref, so_ref, 0)
    o_ref[...] = _gated_head_norm(o, g_ref[...], gn_ref[...])


def _gla_sample(proj, logg, g_norm, j, state, *, row0, batch, heads, dk, dv, bb):
    rb0 = row0 // bb
    kcol = lambda off: pl.BlockSpec((bb, dk), lambda b, h: (rb0 + b, off + h))
    vcol = lambda off: pl.BlockSpec((bb, dv), lambda b, h: (rb0 + b, off + h))
    v_off = 2 * heads * dk // dv
    return pl.pallas_call(
        functools.partial(_gla_sample_kernel, q_scale=dk ** -0.5),
        grid=(batch // bb, heads),
        in_specs=[
            kcol(0), kcol(heads), vcol(v_off), vcol(v_off + heads), kcol(0),
            pl.BlockSpec((None, 1, dv), lambda b, h: (j, 0, 0)),
            pl.BlockSpec((None, bb, 1, dk, dv), lambda b, h: (j, b, h, 0, 0)),
        ],
        out_specs=[
            pl.BlockSpec((bb, dv), lambda b, h: (b, h)),
            pl.BlockSpec((bb, 1, dk, dv), lambda b, h: (b, h, 0, 0)),
        ],
        out_shape=[
            jax.ShapeDtypeStruct((batch, heads * dv), f32),
            jax.ShapeDtypeStruct((batch, heads, dk, dv), f32),
        ],
        compiler_params=_params("parallel", "parallel"),
        name="gla_sample",
    )(proj, proj, proj, proj, logg, g_norm, state)


def _gla_gate_kernel(u_ref, w1_ref, w2_ref, b_ref, o_ref):
    low = jnp.dot(u_ref[...], w1_ref[...].astype(bf16), preferred_element_type=f32)
    gk = jnp.dot(low.astype(bf16), w2_ref[...].astype(bf16), preferred_element_type=f32) + b_ref[...]
    log_sig = jnp.minimum(gk, 0.0) - jnp.log1p(jnp.exp(-jnp.abs(gk)))
    o_ref[...] = log_sig / GLA_GATE_NORMALIZER


def _gla_gate(u, w1, w2, bias, *, tm):
    m, k = u.shape
    r = w1.shape[1]
    n = w2.shape[1]
    full = lambda shape: pl.BlockSpec(shape, lambda i: (0, 0))
    return pl.pallas_call(
        _gla_gate_kernel,
        grid=(m // tm,),
        in_specs=[pl.BlockSpec((tm, k), lambda i: (i, 0)), full((k, r)), full((r, n)), full((1, n))],
        out_specs=pl.BlockSpec((tm, n), lambda i: (i, 0)),
        out_shape=jax.ShapeDtypeStruct((m, n), f32),
        compiler_params=_params("parallel"),
        name="gla_gate",
    )(u, w1, w2, bias)


def _pool_prompt_kernel(u_ref, w_ref, sc_ref, o_ref):
    g = pl.program_id(1)
    x = u_ref[...]
    row = lax.broadcasted_iota(jnp.int32, x.shape, 0)
    pos = lax.broadcasted_iota(jnp.int32, (x.shape[0], 1), 0) + 1
    for gi, win in enumerate(POOL_WINDOWS):
        @pl.when(g == gi)
        def _(win=win):
            s = x
            sh = 1
            while sh < win:
                s = s + jnp.where(row >= sh, pltpu.roll(s, sh, axis=0), 0.0)
                sh *= 2
            cnt = jnp.minimum(pos, win).astype(f32)
            d = s / cnt - x
            y = jnp.dot(d.astype(bf16), w_ref[...].astype(bf16), preferred_element_type=f32)
            o_ref[...] = y * sc_ref[...]


def _pool_prompt(u32, w, scale, j, *, batch, seq):
    groups, gc = w.shape[1], w.shape[2]
    return pl.pallas_call(
        _pool_prompt_kernel,
        grid=(batch, groups),
        in_specs=[
            pl.BlockSpec((seq, gc), lambda b, g: (b, g)),
            pl.BlockSpec((None, None, gc, gc), lambda b, g: (j, g, 0, 0)),
            pl.BlockSpec((None, 1, gc), lambda b, g: (j, 0, g)),
        ],
        out_specs=pl.BlockSpec((seq, gc), lambda b, g: (b, g)),
        out_shape=jax.ShapeDtypeStruct((batch * seq, groups * gc), f32),
        compiler_params=_params("parallel", "parallel"),
        name="pool_prompt",
    )(u32, w, scale)


def _pool_sample_kernel(u_ref, buf_ref, w_ref, sc_ref, o_ref):
    g = pl.program_id(0)
    x = u_ref[...]
    nbuf = buf_ref.shape[1]
    for gi, win in enumerate(POOL_WINDOWS):
        @pl.when(g == gi)
        def _(win=win):
            s = x
            for r in range(nbuf - (win - 1), nbuf):
                s = s + buf_ref[:, r, :]
            d = s / float(win) - x
            y = jnp.dot(d.astype(bf16), w_ref[...].astype(bf16), preferred_element_type=f32)
            o_ref[...] = y * sc_ref[...]


def _pool_sample(u32, buf, w, scale, j, *, row0, batch):
    groups, gc = w.shape[1], w.shape[2]
    nbuf = buf.shape[2]
    return pl.pallas_call(
        _pool_sample_kernel,
        grid=(groups,),
        in_specs=[
            pl.BlockSpec((batch, gc), lambda g: (row0 // batch, g)),
            pl.BlockSpec((None, batch, nbuf, gc), lambda g: (j, 0, 0, g)),
            pl.BlockSpec((None, None, gc, gc), lambda g: (j, g, 0, 0)),
            pl.BlockSpec((None, 1, gc), lambda g: (j, 0, g)),
        ],
        out_specs=pl.BlockSpec((batch, gc), lambda g: (0, g)),
        out_shape=jax.ShapeDtypeStruct((batch, groups * gc), f32),
        compiler_params=_params("parallel"),
        name="pool_sample",
    )(u32, buf, w, scale)


def _xattn_prompt_kernel(q_ref, k_ref, v_ref, o_ref, *, heads):
    dh = q_ref.shape[1] // heads
    scale = dh ** -0.5
    for h in range(heads):
        c = slice(h * dh, (h + 1) * dh)
        s = lax.dot_general(q_ref[:, c], k_ref[:, c].astype(bf16), (((1,), (1,)), ((), ())),
                            preferred_element_type=f32) * scale
        p = jnp.exp(s - jnp.max(s, axis=-1, keepdims=True))
        p = p / jnp.sum(p, axis=-1, keepdims=True)
        o = jnp.dot(p.astype(bf16), v_ref[:, c].astype(bf16), preferred_element_type=f32)
        o_ref[:, c] = o.astype(o_ref.dtype)


def _xattn_prompt(q, mem_k, mem_v, *, batch, seq, n_mem, tq):
    d = q.shape[1]
    nq = seq // tq
    kv = pl.BlockSpec((n_mem, d), lambda b, i: (b, 0))
    return pl.pallas_call(
        functools.partial(_xattn_prompt_kernel, heads=XA_HEADS),
        grid=(batch, nq),
        in_specs=[pl.BlockSpec((tq, d), lambda b, i: (b * nq + i, 0)), kv, kv],
        out_specs=pl.BlockSpec((tq, d), lambda b, i: (b * nq + i, 0)),
        out_shape=jax.ShapeDtypeStruct((batch * seq, d), bf16),
        compiler_params=_params("parallel", "parallel"),
        name="xattn_prompt",
    )(q, mem_k, mem_v)


def _xattn_sample_kernel(q_ref, k_ref, v_ref, o_ref, *, heads, bb):
    dh = q_ref.shape[2] // heads
    scale = dh ** -0.5
    for r in range(bb):
        for h in range(heads):
            c = slice(h * dh, (h + 1) * dh)
            qh = q_ref[r, :, c].astype(f32)
            s = jnp.sum(k_ref[r, :, c] * qh, axis=-1, keepdims=True) * scale
            p = jnp.exp(s - jnp.max(s, axis=0, keepdims=True))
            p = p / jnp.sum(p, axis=0, keepdims=True)
            o = jnp.sum(p * v_ref[r, :, c], axis=0, keepdims=True)
            o_ref[r, :, c] = o.astype(o_ref.dtype)


def _xattn_sample(q3, cache_k, cache_v, layer, *, row0, batch, bb):
    d = q3.shape[2]
    n_mem = cache_k.shape[2]
    kv = pl.BlockSpec((None, bb, n_mem, d), lambda b: (layer, b, 0, 0))
    return pl.pallas_call(
        functools.partial(_xattn_sample_kernel, heads=XA_HEADS, bb=bb),
        grid=(batch // bb,),
        in_specs=[pl.BlockSpec((bb, 1, d), lambda b: (row0 // bb + b, 0, 0)), kv, kv],
        out_specs=pl.BlockSpec((bb, 1, d), lambda b: (b, 0, 0)),
        out_shape=jax.ShapeDtypeStruct((batch, 1, d), bf16),
        compiler_params=_params("parallel"),
        name="xattn_sample",
    )(q3, cache_k, cache_v)


def kernel(x_prompt, x_sample, state_hgrn, state_gla, state_pool, cache_mem_k, cache_mem_v, mem_prompt, norm_gains, hgrn_w_in, hgrn_lb, hgrn_g_norm, hgrn_w_o, gla_w_in, gla_w_gk1, gla_w_gk2, gla_b_gk, gla_g_norm, gla_w_o, pool_w, pool_scale, mem_norm, xa_w_q, xa_w_k, xa_w_v, xa_w_o, mlp_w_up, mlp_w_down):
    batch, seq, d = x_prompt.shape
    dec_batch = x_sample.shape[0]
    depth = norm_gains.shape[0]
    n_mem = mem_prompt.shape[1]
    hg_heads = state_hgrn.shape[2]
    gla_dk, gla_dv = state_gla.shape[3], state_gla.shape[4]
    n_prompt = batch * seq
    m_rows = n_prompt + dec_batch
    tm = m_rows // 10
    assert tm * 10 == m_rows and tm % 16 == 0

    x = jnp.concatenate([x_prompt.reshape(n_prompt, d), x_sample.reshape(dec_batch, d)], axis=0)
    mem = mem_prompt.reshape(batch * n_mem, d)
    cache_k = cache_mem_k.reshape(depth, dec_batch, n_mem, d)
    cache_v = cache_mem_v.reshape(depth, dec_batch, n_mem, d)
    gain = lambda i, n: norm_gains[i, n][None, :]

    lbs = _lower_bounds(hgrn_lb)[:, None, :]
    hgrn_g_norm = hgrn_g_norm[:, None, :]
    gla_g_norm = gla_g_norm[:, None, :]
    pool_scale = pool_scale[:, None, :]
    hgrn_w_o_b = _cast_bf16(hgrn_w_o)
    gla_w_o_b = _cast_bf16(gla_w_o)
    xa_w_o_b = _cast_bf16(xa_w_o)
    mlp_w_down_b = _cast_bf16(mlp_w_down)
    rank = gla_w_gk1.shape[2]
    gk1 = jnp.pad(gla_w_gk1, ((0, 0), (0, 0), (0, 128 - rank)))
    gk2 = jnp.pad(gla_w_gk2, ((0, 0), (0, 128 - rank), (0, 0)))

    new_h_p, new_h_s, new_g_p, new_g_s, new_p_p, new_p_s, mem_ks, mem_vs = [], [], [], [], [], [], [], []
    u = _rmsnorm(x, gain(0, 0), bf16, tm)
    for i in range(depth):
        j, kind = i // 3, i % 3
        if kind == 0:
            proj = _mm_ws(u, hgrn_w_in, j, tm=tm, tn=1024)
            o_p, h_p = _hgrn_prompt(proj, lbs, i, hgrn_g_norm, j, batch=batch, seq=seq, heads=hg_heads, lt=512)
            o_s, h_s = _hgrn_sample(proj, lbs, i, hgrn_g_norm, j, state_hgrn, row0=n_prompt,
                                    batch=dec_batch, heads=hg_heads, bb=8, hb=4)
            new_h_p.append(h_p)
            new_h_s.append(h_s)
            x, u = _mm_res(jnp.concatenate([o_p, o_s.astype(bf16)], axis=0), hgrn_w_o_b, j, x, gain(i, 1), gain(i, 2),
                           tm=tm, tk=512)
        elif kind == 1:
            proj = _mm_ws(u, gla_w_in, j, tm=tm, tn=1024)
            logg = _gla_gate(u, gk1[j], gk2[j], gla_b_gk[j][None, :], tm=tm)
            o_p, g_p = _gla_prompt(proj, logg, gla_g_norm, j, batch=batch, seq=seq, heads=GLA_HEADS,
                                   dk=gla_dk, dv=gla_dv, lt=512)
            o_s, g_s = _gla_sample(proj, logg, gla_g_norm, j, state_gla, row0=n_prompt, batch=dec_batch,
                                   heads=GLA_HEADS, dk=gla_dk, dv=gla_dv, bb=8)
            new_g_p.append(g_p)
            new_g_s.append(g_s)
            x, u = _mm_res(jnp.concatenate([o_p, o_s.astype(bf16)], axis=0), gla_w_o_b, j, x, gain(i, 1), gain(i, 2),
                           tm=tm, tk=512)
        else:
            u32 = _rmsnorm(x, gain(i, 0), f32, tm)
            m_p = _pool_prompt(u32, pool_w, pool_scale, j, batch=batch, seq=seq)
            m_s = _pool_sample(u32, state_pool, pool_w, pool_scale, j, row0=n_prompt, batch=dec_batch)
            nbuf = state_pool.shape[2]
            new_p_p.append(u32[:n_prompt].reshape(batch, seq, d)[:, seq - nbuf:])
            new_p_s.append(jnp.concatenate([state_pool[j][:, 1:], u32[n_prompt:][:, None, :]], axis=1))
            x, u = _add_norm(x, jnp.concatenate([m_p, m_s], axis=0), gain(i, 1), gain(i, 2), tm)

        q = _mm_ws(u, xa_w_q, i, tm=tm, tn=1024, out_dtype=bf16)
        mem_n = _rmsnorm(mem, mem_norm[i][None, :], bf16, 512)
        mem_k = _mm_ws(mem_n, xa_w_k, i, tm=512, tn=1024)
        mem_v = _mm_ws(mem_n, xa_w_v, i, tm=512, tn=1024)
        mem_ks.append(mem_k)
        mem_vs.append(mem_v)
        a_p = _xattn_prompt(q, mem_k, mem_v, batch=batch, seq=seq, n_mem=n_mem, tq=512)
        a_s = _xattn_sample(q.reshape(m_rows, 1, d), cache_k, cache_v, i, row0=n_prompt, batch=dec_batch, bb=2)
        x, u = _mm_res(jnp.concatenate([a_p, a_s.reshape(dec_batch, d)], axis=0), xa_w_o_b, i, x,
                       gain(i, 3), gain(i, 4), tm=tm, tk=512)

        h = _mm_ws(u, mlp_w_up, i, tm=tm, tn=1024, out_dtype=bf16, act="relu2")
        g_next = gain(i + 1, 0) if i + 1 < depth else None
        x, u = _mm_res(h, mlp_w_down_b, i, x, gain(i, 5), g_next, tm=tm, tk=512)

    kv_shape = (depth, batch, n_mem, XA_HEADS, d // XA_HEADS)
    return (
        x[:n_prompt].reshape(batch, seq, d),
        x[n_prompt:].reshape(dec_batch, 1, d),
        jnp.stack(new_h_p),
        jnp.stack(new_h_s),
        jnp.stack(new_g_p),
        jnp.stack(new_g_s),
        jnp.stack(new_p_p),
        jnp.stack(new_p_s),
        jnp.stack(mem_ks).reshape(kv_shape),
        jnp.stack(mem_vs).reshape(kv_shape),
    )
```

```python
import functools

import jax
import jax.numpy as jnp
from jax import lax
from jax.experimental import pallas as pl
from jax.experimental.pallas import tpu as pltpu

f32 = jnp.float32
bf16 = jnp.bfloat16

EPS = 1e-6
LOG2E = 1.4426950408889634
POOL_WINDOWS = (2, 4, 8, 16)
GLA_GATE_NORMALIZER = 16.0
XA_HEADS = 4
HG_DK = 128
GLA_HEADS = 4
CHUNK = 32
SUBLANES = 8
TOKEN_TILES_WS = 10
TOKEN_TILES_RES = 13
VMEM_LIMIT = 56 * 1024 * 1024


def _params(*sem, vmem=VMEM_LIMIT):
    return pltpu.CompilerParams(dimension_semantics=sem, vmem_limit_bytes=vmem)


def _rms_scale(x):
    return lax.rsqrt(jnp.mean(x * x, axis=-1, keepdims=True) + EPS)


def _silu(x):
    return x / (1.0 + jnp.exp(-x))


def _rmsnorm_kernel(x_ref, g_ref, o_ref):
    x = x_ref[...]
    o_ref[...] = (x * _rms_scale(x) * g_ref[...]).astype(o_ref.dtype)


def _rmsnorm(x, g, out_dtype, tm):
    m, d = x.shape
    return pl.pallas_call(
        _rmsnorm_kernel,
        grid=(m // tm,),
        in_specs=[pl.BlockSpec((tm, d), lambda i: (i, 0)), pl.BlockSpec((1, d), lambda i: (0, 0))],
        out_specs=pl.BlockSpec((tm, d), lambda i: (i, 0)),
        out_shape=jax.ShapeDtypeStruct((m, d), out_dtype),
        compiler_params=_params("parallel"),
        name="rmsnorm",
    )(x, g)


def _add_norm_kernel(x_ref, m_ref, gp_ref, gn_ref, xo_ref, uo_ref):
    m = m_ref[...]
    xn = x_ref[...] + m * _rms_scale(m) * gp_ref[...]
    xo_ref[...] = xn
    uo_ref[...] = (xn * _rms_scale(xn) * gn_ref[...]).astype(uo_ref.dtype)


def _add_norm(x, m, g_post, g_next, tm):
    rows, d = x.shape
    row = pl.BlockSpec((tm, d), lambda i: (i, 0))
    gain = pl.BlockSpec((1, d), lambda i: (0, 0))
    return pl.pallas_call(
        _add_norm_kernel,
        grid=(rows // tm,),
        in_specs=[row, row, gain, gain],
        out_specs=[row, row],
        out_shape=[jax.ShapeDtypeStruct((rows, d), f32), jax.ShapeDtypeStruct((rows, d), bf16)],
        compiler_params=_params("parallel"),
        name="add_norm",
    )(x, m, g_post, g_next)


def _cast_kernel(x_ref, o_ref):
    o_ref[...] = x_ref[...].astype(o_ref.dtype)


def _cast_bf16(w, tr=512):
    nl, k, n = w.shape
    spec = pl.BlockSpec((None, tr, n), lambda l, i: (l, i, 0))
    return pl.pallas_call(
        _cast_kernel,
        grid=(nl, k // tr),
        in_specs=[spec],
        out_specs=spec,
        out_shape=jax.ShapeDtypeStruct(w.shape, bf16),
        compiler_params=_params("parallel", "parallel"),
        name="cast_bf16",
    )(w)


def _lower_bounds_kernel(l_ref, o_ref):
    z = l_ref[...]
    p = jnp.exp(z - jnp.max(z, axis=0, keepdims=True))
    p = p / jnp.sum(p, axis=0, keepdims=True)
    acc = jnp.zeros_like(p[0:1])
    for i in range(z.shape[0]):
        acc = acc + p[i:i + 1]
        o_ref[i:i + 1, :] = acc - p[0:1]


def _lower_bounds(lb_logits):
    return pl.pallas_call(
        _lower_bounds_kernel,
        out_shape=jax.ShapeDtypeStruct(lb_logits.shape, f32),
        name="hgrn_lower_bounds",
    )(lb_logits)


def _mm_ws_kernel(x_ref, w_ref, o_ref, wbf_ref, *, act):
    @pl.when(pl.program_id(1) == 0)
    def _():
        wbf_ref[...] = w_ref[...].astype(bf16)

    acc = jnp.dot(x_ref[...], wbf_ref[...], preferred_element_type=f32)
    if act == "relu2":
        acc = jnp.square(jnp.maximum(acc, 0.0))
    o_ref[...] = acc.astype(o_ref.dtype)


def _mm_ws(x, w, layer, *, tm, tn, out_dtype=f32, act=None):
    m, k = x.shape
    n = w.shape[2]
    return pl.pallas_call(
        functools.partial(_mm_ws_kernel, act=act),
        grid=(n // tn, m // tm),
        in_specs=[
            pl.BlockSpec((tm, k), lambda j, i: (i, 0)),
            pl.BlockSpec((None, k, tn), lambda j, i: (layer, 0, j)),
        ],
        out_specs=pl.BlockSpec((tm, tn), lambda j, i: (i, j)),
        out_shape=jax.ShapeDtypeStruct((m, n), out_dtype),
        scratch_shapes=[pltpu.VMEM((k, tn), bf16)],
        compiler_params=_params("arbitrary", "arbitrary"),
        name="matmul_ws",
    )(x, w)


def _mm_res_kernel(a_ref, w_ref, x_ref, gp_ref, *rest, nk, with_next):
    rest = list(rest)
    gn_ref = rest.pop(0) if with_next else None
    xo_ref = rest.pop(0)
    uo_ref = rest.pop(0) if with_next else None

    def finish(m):
        xn = x_ref[...] + m * _rms_scale(m) * gp_ref[...]
        xo_ref[...] = xn
        if with_next:
            uo_ref[...] = (xn * _rms_scale(xn) * gn_ref[...]).astype(uo_ref.dtype)

    if nk == 1:
        finish(jnp.dot(a_ref[...], w_ref[...], preferred_element_type=f32))
        return
    acc_ref = rest.pop(0)
    kk = pl.program_id(1)

    @pl.when(kk == 0)
    def _():
        acc_ref[...] = jnp.zeros_like(acc_ref)

    acc_ref[...] += jnp.dot(a_ref[...], w_ref[...], preferred_element_type=f32)

    @pl.when(kk == nk - 1)
    def _():
        finish(acc_ref[...])


def _mm_res(a, w, layer, x, g_post, g_next, *, tm, tk):
    m, k = a.shape
    d = w.shape[2]
    nk = k // tk
    with_next = g_next is not None
    row = pl.BlockSpec((tm, d), lambda i, kk: (i, 0))
    gain = pl.BlockSpec((1, d), lambda i, kk: (0, 0))
    in_specs = [
        pl.BlockSpec((tm, tk), lambda i, kk: (i, kk)),
        pl.BlockSpec((None, tk, d), lambda i, kk: (layer, kk, 0)),
        row,
        gain,
    ]
    args = [a, w, x, g_post]
    out_specs = [row]
    out_shape = [jax.ShapeDtypeStruct((m, d), f32)]
    if with_next:
        in_specs.append(gain)
        args.append(g_next)
        out_specs.append(row)
        out_shape.append(jax.ShapeDtypeStruct((m, d), bf16))
    res = pl.pallas_call(
        functools.partial(_mm_res_kernel, nk=nk, with_next=with_next),
        grid=(m // tm, nk),
        in_specs=in_specs,
        out_specs=out_specs,
        out_shape=out_shape,
        scratch_shapes=[pltpu.VMEM((tm, d), f32)] if nk > 1 else [],
        compiler_params=_params("parallel", "arbitrary"),
        name="matmul_residual_norm",
    )(*args)
    return (res[0], res[1]) if with_next else (res[0], None)


def _segmented_cumsum(x, seg):
    pos = lax.broadcasted_iota(jnp.int32, x.shape, 0) & (seg - 1)
    sh = 1
    while sh < seg:
        x = x + jnp.where(pos >= sh, pltpu.roll(x, sh, axis=0), 0.0)
        sh *= 2
    return x


def _recurrence_tile(q, k, v, logg, st_ref):
    r, dk = q.shape
    dv = v.shape[1]
    c, g8 = CHUNK, SUBLANES
    nc, ng = r // c, c // g8
    n8 = nc * ng
    b = _segmented_cumsum(logg * LOG2E, c)
    q3, k3, b3 = (x.reshape(nc, c, dk) for x in (q, k, b))
    v3 = v.astype(bf16).reshape(nc, c, dv)
    last = b3[:, c - 1:c, :]
    qe = (q3 * jnp.exp2(b3)).astype(bf16)
    ke = (k3 * jnp.exp2(last - b3)).astype(bf16)

    qg, kg, bg = (x.reshape(n8, g8, dk) for x in (q, k, b))
    lane = lax.broadcasted_iota(jnp.int32, (n8, g8, c), 2)
    group = lax.broadcasted_iota(jnp.int32, (n8, g8, c), 0) & (ng - 1)
    rel = lane - group * g8
    row = lax.broadcasted_iota(jnp.int32, (n8, g8, c), 1)
    diag = jnp.zeros((n8, g8, c), f32)
    for j in range(g8):
        w = jnp.exp2(bg - bg[:, j:j + 1, :])
        col = jnp.sum(qg * w * kg[:, j:j + 1, :], axis=-1, keepdims=True)
        diag = jnp.where(rel == j, col, diag)
    diag = jnp.where(rel <= row, diag, 0.0)

    srow = lax.broadcasted_iota(jnp.int32, (nc, c, dk), 1)
    blocks = [jnp.zeros((nc, g8, c), f32)]
    for i in range(1, ng):
        lo = i * g8
        anchor = b3[:, lo:lo + 1, :]
        qi = (q3[:, lo:lo + g8, :] * jnp.exp2(b3[:, lo:lo + g8, :] - anchor)).astype(bf16)
        ki = jnp.where(srow < lo, k3 * jnp.exp2(anchor - b3), 0.0).astype(bf16)
        blocks.append(jnp.einsum("ctk,csk->cts", qi, ki, preferred_element_type=f32))
    a = jnp.concatenate(blocks, axis=1) + diag.reshape(nc, c, c)

    o_intra = jnp.einsum("cts,csv->ctv", a.astype(bf16), v3, preferred_element_type=f32)
    kvt = jnp.einsum("csv,csk->cvk", v3, ke, preferred_element_type=f32)
    st = st_ref[...]
    o_inter = []
    for ci in range(nc):
        o_inter.append(lax.dot_general(qe[ci], st.astype(bf16), (((1,), (1,)), ((), ())),
                                       preferred_element_type=f32))
        st = st * jnp.exp2(last[ci]) + kvt[ci]
    st_ref[...] = st
    return (o_intra + jnp.stack(o_inter)).reshape(r, dv)


def _gated_head_norm(o, gate, gn):
    return o * _rms_scale(o) * gn * _silu(gate)


def _hgrn_features(q_raw, fz, lb):
    e = jnp.exp(-jnp.abs(fz))
    r = 1.0 / (1.0 + e)
    nonneg = fz >= 0
    sig = jnp.where(nonneg, r, e * r)
    sig_neg = jnp.where(nonneg, e * r, r)
    log_sig = jnp.minimum(fz, 0.0) - jnp.log1p(e)
    logf = jnp.where(lb > 0, jnp.log(lb + (1.0 - lb) * sig), log_sig)
    k = (1.0 - lb) * sig_neg
    return _silu(q_raw), k, logf


def _hgrn_prompt_kernel(q_ref, fz_ref, v_ref, g_ref, lb_ref, gn_ref, o_ref, so_ref, s_ref):
    t = pl.program_id(2)

    @pl.when(t == 0)
    def _():
        s_ref[...] = jnp.zeros_like(s_ref)

    q, k, logf = _hgrn_features(q_ref[...], fz_ref[...], lb_ref[...])
    o = _recurrence_tile(q, k, v_ref[...], logf, s_ref)
    o_ref[...] = _gated_head_norm(o, g_ref[...], gn_ref[...]).astype(o_ref.dtype)

    @pl.when(t == pl.num_programs(2) - 1)
    def _():
        so_ref[...] = jnp.transpose(s_ref[...])


def _hgrn_prompt(proj, lbs, layer, g_norm, j, *, batch, seq, heads, lt):
    dk = HG_DK
    dv = g_norm.shape[-1]
    nt = seq // lt
    col = lambda off: pl.BlockSpec((lt, dk), lambda b, h, t: (b * nt + t, off + h))
    return pl.pallas_call(
        _hgrn_prompt_kernel,
        grid=(batch, heads, nt),
        in_specs=[
            col(0), col(heads), col(2 * heads), col(3 * heads),
            pl.BlockSpec((None, 1, dk), lambda b, h, t: (layer, 0, h)),
            pl.BlockSpec((None, 1, dv), lambda b, h, t: (j, 0, 0)),
        ],
        out_specs=[
            pl.BlockSpec((lt, dv), lambda b, h, t: (b * nt + t, h)),
            pl.BlockSpec((None, None, dk, dv), lambda b, h, t: (b, h, 0, 0)),
        ],
        out_shape=[
            jax.ShapeDtypeStruct((proj.shape[0], heads * dv), bf16),
            jax.ShapeDtypeStruct((batch, heads, dk, dv), f32),
        ],
        scratch_shapes=[pltpu.VMEM((dv, dk), f32)],
        compiler_params=_params("parallel", "parallel", "arbitrary"),
        name="hgrn_prompt",
    )(proj, proj, proj, proj, lbs, g_norm)


def _gla_prompt_kernel(q_ref, k_ref, v_ref, g_ref, lg_ref, gn_ref, o_ref, so_ref, s_ref, *, q_scale):
    t = pl.program_id(2)

    @pl.when(t == 0)
    def _():
        s_ref[...] = jnp.zeros_like(s_ref)

    o = _recurrence_tile(q_ref[...] * q_scale, k_ref[...], v_ref[...], lg_ref[...], s_ref)
    o_ref[...] = _gated_head_norm(o, g_ref[...], gn_ref[...]).astype(o_ref.dtype)

    @pl.when(t == pl.num_programs(2) - 1)
    def _():
        so_ref[...] = jnp.transpose(s_ref[...])


def _gla_prompt(proj, logg, g_norm, j, *, batch, seq, heads, dk, dv, lt):
    nt = seq // lt
    kcol = lambda off: pl.BlockSpec((lt, dk), lambda b, h, t: (b * nt + t, off + h))
    vcol = lambda off: pl.BlockSpec((lt, dv), lambda b, h, t: (b * nt + t, off + h))
    v_off = 2 * heads * dk // dv
    return pl.pallas_call(
        functools.partial(_gla_prompt_kernel, q_scale=dk ** -0.5),
        grid=(batch, heads, nt),
        in_specs=[
            kcol(0), kcol(heads), vcol(v_off), vcol(v_off + heads), kcol(0),
            pl.BlockSpec((None, 1, dv), lambda b, h, t: (j, 0, 0)),
        ],
        out_specs=[
            pl.BlockSpec((lt, dv), lambda b, h, t: (b * nt + t, h)),
            pl.BlockSpec((None, None, dk, dv), lambda b, h, t: (b, h, 0, 0)),
        ],
        out_shape=[
            jax.ShapeDtypeStruct((proj.shape[0], heads * dv), bf16),
            jax.ShapeDtypeStruct((batch, heads, dk, dv), f32),
        ],
        scratch_shapes=[pltpu.VMEM((dv, dk), f32)],
        compiler_params=_params("parallel", "parallel", "arbitrary"),
        name="gla_prompt",
    )(proj, proj, proj, proj, logg, g_norm)


def _state_step(q, k, v, logg, s_ref, so_ref, hh):
    bb, dk = q.shape
    dv = v.shape[1]
    cols = jnp.transpose(jnp.concatenate([jnp.exp(logg), k, q], axis=0))
    outs = []
    for r in range(bb):
        e_col = jnp.broadcast_to(cols[:, r:r + 1], (dk, dv))
        k_col = jnp.broadcast_to(cols[:, bb + r:bb + r + 1], (dk, dv))
        q_col = jnp.broadcast_to(cols[:, 2 * bb + r:2 * bb + r + 1], (dk, dv))
        s_new = e_col * s_ref[r, hh] + k_col * v[r:r + 1, :]
        so_ref[r, hh] = s_new
        outs.append(jnp.sum(q_col * s_new, axis=0, keepdims=True))
    return jnp.concatenate(outs, axis=0)


def _hgrn_sample_kernel(q_ref, fz_ref, v_ref, g_ref, lb_ref, gn_ref, s_ref, *rest, hb, j, first):
    o_ref, so_ref = rest[-2:]
    if first:
        for layer in range(so_ref.shape[0]):
            if layer != j:
                so_ref[layer] = jnp.zeros(so_ref.shape[1:], f32)
        so_ref = so_ref.at[j]
    dk = HG_DK
    dv = gn_ref.shape[1]
    gn = gn_ref[...]
    for hh in range(hb):
        kc = slice(hh * dk, (hh + 1) * dk)
        vc = slice(hh * dv, (hh + 1) * dv)
        q, k, logf = _hgrn_features(q_ref[:, kc], fz_ref[:, kc], lb_ref[:, kc])
        o = _state_step(q, k, v_ref[:, vc], logf, s_ref, so_ref, hh)
        o_ref[:, vc] = _gated_head_norm(o, g_ref[:, vc], gn)


def _hgrn_sample(proj, lbs, layer, g_norm, j, state, new_states, *, row0, batch, heads, bb, hb):
    dk = HG_DK
    dv = g_norm.shape[-1]
    rb0 = row0 // bb
    hblocks = heads // hb
    n_layers = state.shape[0]
    first = new_states is None
    col = lambda off: pl.BlockSpec((bb, hb * dk), lambda b, h: (rb0 + b, off + h))
    st = pl.BlockSpec((None, bb, hb, dk, dv), lambda b, h: (j, b, h, 0, 0))
    in_specs = [
        col(0), col(hblocks), col(2 * hblocks), col(3 * hblocks),
        pl.BlockSpec((None, 1, hb * dk), lambda b, h: (layer, 0, h)),
        pl.BlockSpec((None, 1, dv), lambda b, h: (j, 0, 0)),
        st,
    ]
    args = [proj, proj, proj, proj, lbs, g_norm, state]
    if first:
        st_out = pl.BlockSpec((n_layers, bb, hb, dk, dv), lambda b, h: (0, b, h, 0, 0))
        aliases = {}
    else:
        st_out = st
        in_specs.append(pl.BlockSpec(memory_space=pl.ANY))
        args.append(new_states)
        aliases = {len(args) - 1: 1}
    return pl.pallas_call(
        functools.partial(_hgrn_sample_kernel, hb=hb, j=j, first=first),
        grid=(batch // bb, hblocks),
        in_specs=in_specs,
        out_specs=[pl.BlockSpec((bb, hb * dv), lambda b, h: (b, h)), st_out],
        out_shape=[
            jax.ShapeDtypeStruct((batch, heads * dv), f32),
            jax.ShapeDtypeStruct(state.shape, f32),
        ],
        input_output_aliases=aliases,
        compiler_params=_params("parallel", "parallel"),
        name="hgrn_sample",
    )(*args)


def _gla_sample_kernel(q_ref, k_ref, v_ref, g_ref, lg_ref, gn_ref, s_ref, o_ref, so_ref, *, q_scale):
    o = _state_step(q_ref[...] * q_scale, k_ref[...], v_ref[...], lg_ref[...], s_ref, so_ref, 0)
    o_ref[...] = _gated_head_norm(o, g_ref[...], gn_ref[...])


def _gla_sample(proj, logg, g_norm, j, state, *, row0, batch, heads, dk, dv, bb):
    rb0 = row0 // bb
    kcol = lambda off: pl.BlockSpec((bb, dk), lambda b, h: (rb0 + b, off + h))
    vcol = lambda off: pl.BlockSpec((bb, dv), lambda b, h: (rb0 + b, off + h))
    v_off = 2 * heads * dk // dv
    return pl.pallas_call(
        functools.partial(_gla_sample_kernel, q_scale=dk ** -0.5),
        grid=(batch // bb, heads),
        in_specs=[
            kcol(0), kcol(heads), vcol(v_off), vcol(v_off + heads), kcol(0),
            pl.BlockSpec((None, 1, dv), lambda b, h: (j, 0, 0)),
            pl.BlockSpec((None, bb, 1, dk, dv), lambda b, h: (j, b, h, 0, 0)),
        ],
        out_specs=[
            pl.BlockSpec((bb, dv), lambda b, h: (b, h)),
            pl.BlockSpec((bb, 1, dk, dv), lambda b, h: (b, h, 0, 0)),
        ],
        out_shape=[
            jax.ShapeDtypeStruct((batch, heads * dv), f32),
            jax.ShapeDtypeStruct((batch, heads, dk, dv), f32),
        ],
        compiler_params=_params("parallel", "parallel"),
        name="gla_sample",
    )(proj, proj, proj, proj, logg, g_norm, state)


def _gla_gate_kernel(u_ref, w1_ref, w2_ref, b_ref, o_ref):
    low = jnp.dot(u_ref[...], w1_ref[...].astype(bf16), preferred_element_type=f32)
    gk = jnp.dot(low.astype(bf16), w2_ref[...].astype(bf16), preferred_element_type=f32) + b_ref[...]
    log_sig = jnp.minimum(gk, 0.0) - jnp.log1p(jnp.exp(-jnp.abs(gk)))
    o_ref[...] = log_sig / GLA_GATE_NORMALIZER


def _gla_gate(u, w1, w2, bias, *, tm):
    m, k = u.shape
    r = w1.shape[1]
    n = w2.shape[1]
    full = lambda shape: pl.BlockSpec(shape, lambda i: (0, 0))
    return pl.pallas_call(
        _gla_gate_kernel,
        grid=(m // tm,),
        in_specs=[pl.BlockSpec((tm, k), lambda i: (i, 0)), full((k, r)), full((r, n)), full((1, n))],
        out_specs=pl.BlockSpec((tm, n), lambda i: (i, 0)),
        out_shape=jax.ShapeDtypeStruct((m, n), f32),
        compiler_params=_params("parallel"),
        name="gla_gate",
    )(u, w1, w2, bias)


def _pool_prompt_kernel(u_ref, w_ref, sc_ref, o_ref):
    g = pl.program_id(1)
    x = u_ref[...]
    row = lax.broadcasted_iota(jnp.int32, x.shape, 0)
    pos = lax.broadcasted_iota(jnp.int32, (x.shape[0], 1), 0) + 1
    for gi, win in enumerate(POOL_WINDOWS):
        @pl.when(g == gi)
        def _(win=win):
            s = x
            sh = 1
            while sh < win:
                s = s + jnp.where(row >= sh, pltpu.roll(s, sh, axis=0), 0.0)
                sh *= 2
            cnt = jnp.minimum(pos, win).astype(f32)
            d = s / cnt - x
            y = jnp.dot(d.astype(bf16), w_ref[...].astype(bf16), preferred_element_type=f32)
            o_ref[...] = y * sc_ref[...]


def _pool_prompt(u32, w, scale, j, *, batch, seq):
    groups, gc = w.shape[1], w.shape[2]
    return pl.pallas_call(
        _pool_prompt_kernel,
        grid=(batch, groups),
        in_specs=[
            pl.BlockSpec((seq, gc), lambda b, g: (b, g)),
            pl.BlockSpec((None, None, gc, gc), lambda b, g: (j, g, 0, 0)),
            pl.BlockSpec((None, 1, gc), lambda b, g: (j, 0, g)),
        ],
        out_specs=pl.BlockSpec((seq, gc), lambda b, g: (b, g)),
        out_shape=jax.ShapeDtypeStruct((u32.shape[0], groups * gc), f32),
        compiler_params=_params("parallel", "parallel"),
        name="pool_prompt",
    )(u32, w, scale)


def _pool_sample_kernel(u_ref, buf_ref, w_ref, sc_ref, o_ref):
    g = pl.program_id(0)
    x = u_ref[...]
    nbuf = buf_ref.shape[1]
    for gi, win in enumerate(POOL_WINDOWS):
        @pl.when(g == gi)
        def _(win=win):
            s = x
            for r in range(nbuf - (win - 1), nbuf):
                s = s + buf_ref[:, r, :]
            d = s / float(win) - x
            y = jnp.dot(d.astype(bf16), w_ref[...].astype(bf16), preferred_element_type=f32)
            o_ref[...] = y * sc_ref[...]


def _pool_sample(u32, buf, w, scale, j, *, row0, batch):
    groups, gc = w.shape[1], w.shape[2]
    nbuf = buf.shape[2]
    return pl.pallas_call(
        _pool_sample_kernel,
        grid=(groups,),
        in_specs=[
            pl.BlockSpec((batch, gc), lambda g: (row0 // batch, g)),
            pl.BlockSpec((None, batch, nbuf, gc), lambda g: (j, 0, 0, g)),
            pl.BlockSpec((None, None, gc, gc), lambda g: (j, g, 0, 0)),
            pl.BlockSpec((None, 1, gc), lambda g: (j, 0, g)),
        ],
        out_specs=pl.BlockSpec((batch, gc), lambda g: (0, g)),
        out_shape=jax.ShapeDtypeStruct((batch, groups * gc), f32),
        compiler_params=_params("parallel"),
        name="pool_sample",
    )(u32, buf, w, scale)


def _xattn_prompt_kernel(q_ref, k_ref, v_ref, o_ref, *, heads):
    dh = q_ref.shape[1] // heads
    scale = dh ** -0.5
    for h in range(heads):
        c = slice(h * dh, (h + 1) * dh)
        s = lax.dot_general(q_ref[:, c], k_ref[:, c].astype(bf16), (((1,), (1,)), ((), ())),
                            preferred_element_type=f32) * scale
        p = jnp.exp(s - jnp.max(s, axis=-1, keepdims=True))
        p = p / jnp.sum(p, axis=-1, keepdims=True)
        o = jnp.dot(p.astype(bf16), v_ref[:, c].astype(bf16), preferred_element_type=f32)
        o_ref[:, c] = o.astype(o_ref.dtype)


def _xattn_prompt(q, mem_k, mem_v, *, batch, seq, n_mem, tq):
    d = q.shape[1]
    nq = seq // tq
    kv = pl.BlockSpec((n_mem, d), lambda b, i: (b, 0))
    return pl.pallas_call(
        functools.partial(_xattn_prompt_kernel, heads=XA_HEADS),
        grid=(batch, nq),
        in_specs=[pl.BlockSpec((tq, d), lambda b, i: (b * nq + i, 0)), kv, kv],
        out_specs=pl.BlockSpec((tq, d), lambda b, i: (b * nq + i, 0)),
        out_shape=jax.ShapeDtypeStruct((q.shape[0], d), bf16),
        compiler_params=_params("parallel", "parallel"),
        name="xattn_prompt",
    )(q, mem_k, mem_v)


def _xattn_sample_kernel(q_ref, k_ref, v_ref, o_ref, *, bb):
    scale = q_ref.shape[2] ** -0.5
    for r in range(bb):
        q = q_ref[r]
        s = jnp.sum(k_ref[r] * q[None], axis=-1, keepdims=True) * scale
        p = jnp.exp(s - jnp.max(s, axis=0, keepdims=True))
        p = p / jnp.sum(p, axis=0, keepdims=True)
        o_ref[r] = jnp.sum(p * v_ref[r], axis=0)


def _xattn_sample(q, cache_k, cache_v, layer, *, bb):
    batch, heads, dh = q.shape
    n_mem = cache_k.shape[2]
    kv = pl.BlockSpec((None, bb, n_mem, heads, dh), lambda b: (layer, b, 0, 0, 0))
    qo = pl.BlockSpec((bb, heads, dh), lambda b: (b, 0, 0))
    return pl.pallas_call(
        functools.partial(_xattn_sample_kernel, bb=bb),
        grid=(batch // bb,),
        in_specs=[qo, kv, kv],
        out_specs=qo,
        out_shape=jax.ShapeDtypeStruct((batch, heads, dh), f32),
        compiler_params=_params("parallel"),
        name="xattn_sample",
    )(q, cache_k, cache_v)


def kernel(x_prompt, x_sample, state_hgrn, state_gla, state_pool, cache_mem_k, cache_mem_v, mem_prompt, norm_gains, hgrn_w_in, hgrn_lb, hgrn_g_norm, hgrn_w_o, gla_w_in, gla_w_gk1, gla_w_gk2, gla_b_gk, gla_g_norm, gla_w_o, pool_w, pool_scale, mem_norm, xa_w_q, xa_w_k, xa_w_v, xa_w_o, mlp_w_up, mlp_w_down):
    batch, seq, d = x_prompt.shape
    dec_batch = x_sample.shape[0]
    depth = norm_gains.shape[0]
    n_mem = mem_prompt.shape[1]
    hg_heads = state_hgrn.shape[2]
    gla_dk, gla_dv = state_gla.shape[3], state_gla.shape[4]
    n_prompt = batch * seq
    m_rows = n_prompt + dec_batch
    tm = m_rows // TOKEN_TILES_WS
    tr = m_rows // TOKEN_TILES_RES
    assert tm * TOKEN_TILES_WS == m_rows and tm % 16 == 0
    assert tr * TOKEN_TILES_RES == m_rows and tr % 16 == 0
    dh = d // XA_HEADS

    x = jnp.concatenate([x_prompt.reshape(n_prompt, d), x_sample.reshape(dec_batch, d)], axis=0)
    mem = mem_prompt.reshape(batch * n_mem, d)
    gain = lambda i, n: norm_gains[i, n][None, :]
    with_sample = lambda full, rows: lax.dynamic_update_slice(full, rows.astype(full.dtype), (n_prompt, 0))

    lbs = _lower_bounds(hgrn_lb)[:, None, :]
    hgrn_g_norm = hgrn_g_norm[:, None, :]
    gla_g_norm = gla_g_norm[:, None, :]
    pool_scale = pool_scale[:, None, :]
    hgrn_w_o_b = _cast_bf16(hgrn_w_o)
    gla_w_o_b = _cast_bf16(gla_w_o)
    xa_w_o_b = _cast_bf16(xa_w_o)
    mlp_w_down_b = _cast_bf16(mlp_w_down)
    rank = gla_w_gk1.shape[2]
    gk1 = jnp.pad(gla_w_gk1, ((0, 0), (0, 0), (0, 128 - rank)))
    gk2 = jnp.pad(gla_w_gk2, ((0, 0), (0, 128 - rank), (0, 0)))

    new_h_p, new_g_p, new_g_s, new_p_p, new_p_s, mem_ks, mem_vs = [], [], [], [], [], [], []
    new_h_s = None
    u = _rmsnorm(x, gain(0, 0), bf16, tm)
    for i in range(depth):
        j, kind = i // 3, i % 3
        if kind == 0:
            proj = _mm_ws(u, hgrn_w_in, j, tm=tm, tn=1024)
            o, h_p = _hgrn_prompt(proj, lbs, i, hgrn_g_norm, j, batch=batch, seq=seq, heads=hg_heads, lt=512)
            o_s, new_h_s = _hgrn_sample(proj, lbs, i, hgrn_g_norm, j, state_hgrn, new_h_s, row0=n_prompt,
                                        batch=dec_batch, heads=hg_heads, bb=8, hb=4)
            new_h_p.append(h_p)
            x, u = _mm_res(with_sample(o, o_s), hgrn_w_o_b, j, x, gain(i, 1), gain(i, 2), tm=tr, tk=d)
        elif kind == 1:
            proj = _mm_ws(u, gla_w_in, j, tm=tm, tn=1024)
            logg = _gla_gate(u, gk1[j], gk2[j], gla_b_gk[j][None, :], tm=tm)
            o, g_p = _gla_prompt(proj, logg, gla_g_norm, j, batch=batch, seq=seq, heads=GLA_HEADS,
                                 dk=gla_dk, dv=gla_dv, lt=256)
            o_s, g_s = _gla_sample(proj, logg, gla_g_norm, j, state_gla, row0=n_prompt, batch=dec_batch,
                                   heads=GLA_HEADS, dk=gla_dk, dv=gla_dv, bb=8)
            new_g_p.append(g_p)
            new_g_s.append(g_s)
            x, u = _mm_res(with_sample(o, o_s), gla_w_o_b, j, x, gain(i, 1), gain(i, 2), tm=tr, tk=d)
        else:
            u32 = _rmsnorm(x, gain(i, 0), f32, tm)
            m = _pool_prompt(u32, pool_w, pool_scale, j, batch=batch, seq=seq)
            m_s = _pool_sample(u32, state_pool, pool_w, pool_scale, j, row0=n_prompt, batch=dec_batch)
            nbuf = state_pool.shape[2]
            new_p_p.append(jnp.stack([u32[(b + 1) * seq - nbuf:(b + 1) * seq] for b in range(batch)]))
            new_p_s.append(jnp.concatenate([state_pool[j][:, 1:], u32[n_prompt:][:, None, :]], axis=1))
            x, u = _add_norm(x, with_sample(m, m_s), gain(i, 1), gain(i, 2), tr)

        q = _mm_ws(u, xa_w_q, i, tm=tm, tn=1024, out_dtype=bf16)
        mem_n = _rmsnorm(mem, mem_norm[i][None, :], bf16, 512)
        mem_k = _mm_ws(mem_n, xa_w_k, i, tm=512, tn=1024)
        mem_v = _mm_ws(mem_n, xa_w_v, i, tm=512, tn=1024)
        mem_ks.append(mem_k)
        mem_vs.append(mem_v)
        a = _xattn_prompt(q, mem_k, mem_v, batch=batch, seq=seq, n_mem=n_mem, tq=512)
        q_s = q[n_prompt:].astype(f32).reshape(dec_batch, XA_HEADS, dh)
        a_s = _xattn_sample(q_s, cache_mem_k, cache_mem_v, i, bb=2)
        x, u = _mm_res(with_sample(a, a_s.reshape(dec_batch, d)), xa_w_o_b, i, x, gain(i, 3), gain(i, 4),
                       tm=tr, tk=d)

        h = _mm_ws(u, mlp_w_up, i, tm=tm, tn=1024, out_dtype=bf16, act="relu2")
        g_next = gain(i + 1, 0) if i + 1 < depth else None
        x, u = _mm_res(h, mlp_w_down_b, i, x, gain(i, 5), g_next, tm=tr, tk=1024)

    kv_shape = (depth, batch, n_mem, XA_HEADS, dh)
    return (
        x[:n_prompt].reshape(batch, seq, d),
        x[n_prompt:].reshape(dec_batch, 1, d),
        jnp.stack(new_h_p),
        new_h_s,
        jnp.stack(new_g_p),
        jnp.stack(new_g_s),
        jnp.stack(new_p_p),
        jnp.stack(new_p_s),
        jnp.stack(mem_ks).reshape(kv_shape),
        jnp.stack(mem_vs).reshape(kv_shape),
    )
```

```python
import functools

import jax
import jax.numpy as jnp
from jax import lax
from jax.experimental import pallas as pl
from jax.experimental.pallas import tpu as pltpu

f32 = jnp.float32
bf16 = jnp.bfloat16

EPS = 1e-6
LOG2E = 1.4426950408889634
POOL_WINDOWS = (2, 4, 8, 16)
GLA_GATE_NORMALIZER = 16.0
XA_HEADS = 4
HG_DK = 128
GLA_HEADS = 4
CHUNK = 32
SUBLANES = 8
BF16_ROWS = 16
TOKEN_TILES_WS = 5
TOKEN_TILES_RES = 13
VMEM_LIMIT = 56 * 1024 * 1024


def _params(*sem, vmem=VMEM_LIMIT):
    return pltpu.CompilerParams(dimension_semantics=sem, vmem_limit_bytes=vmem)


def _rms_scale(x):
    return lax.rsqrt(jnp.mean(x * x, axis=-1, keepdims=True) + EPS)


def _silu(x):
    return x / (1.0 + jnp.exp(-x))


def _rmsnorm_kernel(x_ref, g_ref, o_ref):
    x = x_ref[...]
    o_ref[...] = (x * _rms_scale(x) * g_ref[...]).astype(o_ref.dtype)


def _rmsnorm(x, g, out_dtype, tm):
    m, d = x.shape
    return pl.pallas_call(
        _rmsnorm_kernel,
        grid=(m // tm,),
        in_specs=[pl.BlockSpec((tm, d), lambda i: (i, 0)), pl.BlockSpec((1, d), lambda i: (0, 0))],
        out_specs=pl.BlockSpec((tm, d), lambda i: (i, 0)),
        out_shape=jax.ShapeDtypeStruct((m, d), out_dtype),
        compiler_params=_params("parallel"),
        name="rmsnorm",
    )(x, g)


def _add_norm_kernel(x_ref, m_ref, gp_ref, gn_ref, xo_ref, uo_ref):
    m = m_ref[...]
    xn = x_ref[...] + m * _rms_scale(m) * gp_ref[...]
    xo_ref[...] = xn
    uo_ref[...] = (xn * _rms_scale(xn) * gn_ref[...]).astype(uo_ref.dtype)


def _add_norm(x, m, g_post, g_next, tm):
    rows, d = x.shape
    row = pl.BlockSpec((tm, d), lambda i: (i, 0))
    gain = pl.BlockSpec((1, d), lambda i: (0, 0))
    return pl.pallas_call(
        _add_norm_kernel,
        grid=(rows // tm,),
        in_specs=[row, row, gain, gain],
        out_specs=[row, row],
        out_shape=[jax.ShapeDtypeStruct((rows, d), f32), jax.ShapeDtypeStruct((rows, d), bf16)],
        compiler_params=_params("parallel"),
        name="add_norm",
    )(x, m, g_post, g_next)


def _cast_kernel(x_ref, o_ref):
    o_ref[...] = x_ref[...].astype(o_ref.dtype)


def _cast_bf16(w, tr=512):
    nl, k, n = w.shape
    spec = pl.BlockSpec((None, tr, n), lambda l, i: (l, i, 0))
    return pl.pallas_call(
        _cast_kernel,
        grid=(nl, k // tr),
        in_specs=[spec],
        out_specs=spec,
        out_shape=jax.ShapeDtypeStruct(w.shape, bf16),
        compiler_params=_params("parallel", "parallel"),
        name="cast_bf16",
    )(w)


def _lower_bounds_kernel(l_ref, o_ref):
    z = l_ref[...]
    p = jnp.exp(z - jnp.max(z, axis=0, keepdims=True))
    p = p / jnp.sum(p, axis=0, keepdims=True)
    acc = jnp.zeros_like(p[0:1])
    for i in range(z.shape[0]):
        acc = acc + p[i:i + 1]
        o_ref[i:i + 1, :] = acc - p[0:1]


def _lower_bounds(lb_logits):
    return pl.pallas_call(
        _lower_bounds_kernel,
        out_shape=jax.ShapeDtypeStruct(lb_logits.shape, f32),
        name="hgrn_lower_bounds",
    )(lb_logits)


def _mm_ws_kernel(x_ref, w_ref, o_ref, wbf_ref, *, act):
    @pl.when(pl.program_id(1) == 0)
    def _():
        wbf_ref[...] = w_ref[...].astype(bf16)

    acc = jnp.dot(x_ref[...], wbf_ref[...], preferred_element_type=f32)
    if act == "relu2":
        acc = jnp.square(jnp.maximum(acc, 0.0))
    o_ref[...] = acc.astype(o_ref.dtype)


def _mm_ws(x, w, layer, *, tm, tn, out_dtype=f32, act=None):
    m, k = x.shape
    n = w.shape[2]
    return pl.pallas_call(
        functools.partial(_mm_ws_kernel, act=act),
        grid=(n // tn, m // tm),
        in_specs=[
            pl.BlockSpec((tm, k), lambda j, i: (i, 0)),
            pl.BlockSpec((None, k, tn), lambda j, i: (layer, 0, j)),
        ],
        out_specs=pl.BlockSpec((tm, tn), lambda j, i: (i, j)),
        out_shape=jax.ShapeDtypeStruct((m, n), out_dtype),
        scratch_shapes=[pltpu.VMEM((k, tn), bf16)],
        compiler_params=_params("arbitrary", "arbitrary"),
        name="matmul_ws",
    )(x, w)


def _mm_res_kernel(a_ref, w_ref, x_ref, gp_ref, *rest, nk, with_next):
    rest = list(rest)
    gn_ref = rest.pop(0) if with_next else None
    xo_ref = rest.pop(0)
    uo_ref = rest.pop(0) if with_next else None

    def finish(m):
        xn = x_ref[...] + m * _rms_scale(m) * gp_ref[...]
        xo_ref[...] = xn
        if with_next:
            uo_ref[...] = (xn * _rms_scale(xn) * gn_ref[...]).astype(uo_ref.dtype)

    if nk == 1:
        finish(jnp.dot(a_ref[...], w_ref[...], preferred_element_type=f32))
        return
    acc_ref = rest.pop(0)
    kk = pl.program_id(1)

    @pl.when(kk == 0)
    def _():
        acc_ref[...] = jnp.zeros_like(acc_ref)

    acc_ref[...] += jnp.dot(a_ref[...], w_ref[...], preferred_element_type=f32)

    @pl.when(kk == nk - 1)
    def _():
        finish(acc_ref[...])


def _mm_res(a, w, layer, x, g_post, g_next, *, tm, tk):
    m, k = a.shape
    d = w.shape[2]
    nk = k // tk
    with_next = g_next is not None
    row = pl.BlockSpec((tm, d), lambda i, kk: (i, 0))
    gain = pl.BlockSpec((1, d), lambda i, kk: (0, 0))
    in_specs = [
        pl.BlockSpec((tm, tk), lambda i, kk: (i, kk)),
        pl.BlockSpec((None, tk, d), lambda i, kk: (layer, kk, 0)),
        row,
        gain,
    ]
    args = [a, w, x, g_post]
    out_specs = [row]
    out_shape = [jax.ShapeDtypeStruct((m, d), f32)]
    if with_next:
        in_specs.append(gain)
        args.append(g_next)
        out_specs.append(row)
        out_shape.append(jax.ShapeDtypeStruct((m, d), bf16))
    res = pl.pallas_call(
        functools.partial(_mm_res_kernel, nk=nk, with_next=with_next),
        grid=(m // tm, nk),
        in_specs=in_specs,
        out_specs=out_specs,
        out_shape=out_shape,
        scratch_shapes=[pltpu.VMEM((tm, d), f32)] if nk > 1 else [],
        compiler_params=_params("parallel", "arbitrary"),
        name="matmul_residual_norm",
    )(*args)
    return (res[0], res[1]) if with_next else (res[0], None)


def _segmented_cumsum(x, seg):
    pos = lax.broadcasted_iota(jnp.int32, x.shape, 0) & (seg - 1)
    sh = 1
    while sh < seg:
        x = x + jnp.where(pos >= sh, pltpu.roll(x, sh, axis=0), 0.0)
        sh *= 2
    return x


def _recurrence_tile(q, k, v, logg, st_ref):
    r, dk = q.shape
    dv = v.shape[1]
    c, g8 = CHUNK, SUBLANES
    nc, ng = r // c, c // g8
    n8 = nc * ng
    b = _segmented_cumsum(logg * LOG2E, c)
    q3, k3, b3 = (x.reshape(nc, c, dk) for x in (q, k, b))
    v3 = v.astype(bf16).reshape(nc, c, dv)
    last = b3[:, c - 1:c, :]
    qe = (q3 * jnp.exp2(b3)).astype(bf16)
    ke = (k3 * jnp.exp2(last - b3)).astype(bf16)

    qg, kg, bg = (x.reshape(n8, g8, dk) for x in (q, k, b))
    lane = lax.broadcasted_iota(jnp.int32, (n8, g8, c), 2)
    group = lax.broadcasted_iota(jnp.int32, (n8, g8, c), 0) & (ng - 1)
    rel = lane - group * g8
    row = lax.broadcasted_iota(jnp.int32, (n8, g8, c), 1)
    diag = jnp.zeros((n8, g8, c), f32)
    for j in range(g8):
        w = jnp.exp2(bg - bg[:, j:j + 1, :])
        col = jnp.sum(qg * w * kg[:, j:j + 1, :], axis=-1, keepdims=True)
        diag = jnp.where(rel == j, col, diag)
    diag = jnp.where(rel <= row, diag, 0.0)

    srow = lax.broadcasted_iota(jnp.int32, (nc, c, dk), 1)
    blocks = [jnp.zeros((nc, g8, c), f32)]
    for i in range(1, ng):
        lo = i * g8
        anchor = b3[:, lo:lo + 1, :]
        qi = (q3[:, lo:lo + g8, :] * jnp.exp2(b3[:, lo:lo + g8, :] - anchor)).astype(bf16)
        ki = jnp.where(srow < lo, k3 * jnp.exp2(anchor - b3), 0.0).astype(bf16)
        blocks.append(jnp.einsum("ctk,csk->cts", qi, ki, preferred_element_type=f32))
    a = jnp.concatenate(blocks, axis=1) + diag.reshape(nc, c, c)

    o_intra = jnp.einsum("cts,csv->ctv", a.astype(bf16), v3, preferred_element_type=f32)
    kvt = jnp.einsum("csv,csk->cvk", v3, ke, preferred_element_type=f32)
    st = st_ref[...]
    o_inter = []
    for ci in range(nc):
        o_inter.append(lax.dot_general(qe[ci], st.astype(bf16), (((1,), (1,)), ((), ())),
                                       preferred_element_type=f32))
        st = st * jnp.exp2(last[ci]) + kvt[ci]
    st_ref[...] = st
    return (o_intra + jnp.stack(o_inter)).reshape(r, dv)


def _gated_head_norm(o, gate, gn):
    return o * _rms_scale(o) * gn * _silu(gate)


def _hgrn_features(q_raw, fz, lb):
    e = jnp.exp(-jnp.abs(fz))
    r = 1.0 / (1.0 + e)
    nonneg = fz >= 0
    sig = jnp.where(nonneg, r, e * r)
    sig_neg = jnp.where(nonneg, e * r, r)
    log_sig = jnp.minimum(fz, 0.0) - jnp.log(1.0 + e)
    logf = jnp.where(lb > 0, jnp.log(lb + (1.0 - lb) * sig), log_sig)
    k = (1.0 - lb) * sig_neg
    return _silu(q_raw), k, logf


def _hgrn_prompt_kernel(q_ref, fz_ref, v_ref, g_ref, lb_ref, gn_ref, o_ref, so_ref, s_ref):
    t = pl.program_id(2)

    @pl.when(t == 0)
    def _():
        s_ref[...] = jnp.zeros_like(s_ref)

    q, k, logf = _hgrn_features(q_ref[...], fz_ref[...], lb_ref[...])
    o = _recurrence_tile(q, k, v_ref[...], logf, s_ref)
    o_ref[...] = _gated_head_norm(o, g_ref[...], gn_ref[...]).astype(o_ref.dtype)

    @pl.when(t == pl.num_programs(2) - 1)
    def _():
        so_ref[...] = jnp.transpose(s_ref[...])


def _hgrn_prompt(proj, lbs, layer, g_norm, j, *, batch, seq, heads, lt):
    dk = HG_DK
    dv = g_norm.shape[-1]
    nt = seq // lt
    col = lambda off: pl.BlockSpec((lt, dk), lambda b, h, t: (b * nt + t, off + h))
    return pl.pallas_call(
        _hgrn_prompt_kernel,
        grid=(batch, heads, nt),
        in_specs=[
            col(0), col(heads), col(2 * heads), col(3 * heads),
            pl.BlockSpec((None, 1, dk), lambda b, h, t: (layer, 0, h)),
            pl.BlockSpec((None, 1, dv), lambda b, h, t: (j, 0, 0)),
        ],
        out_specs=[
            pl.BlockSpec((lt, dv), lambda b, h, t: (b * nt + t, h)),
            pl.BlockSpec((None, None, dk, dv), lambda b, h, t: (b, h, 0, 0)),
        ],
        out_shape=[
            jax.ShapeDtypeStruct((proj.shape[0], heads * dv), bf16),
            jax.ShapeDtypeStruct((batch, heads, dk, dv), f32),
        ],
        scratch_shapes=[pltpu.VMEM((dv, dk), f32)],
        compiler_params=_params("parallel", "parallel", "arbitrary"),
        name="hgrn_prompt",
    )(proj, proj, proj, proj, lbs, g_norm)


def _gla_prompt_kernel(q_ref, k_ref, v_ref, g_ref, lg_ref, gn_ref, o_ref, so_ref, s_ref, *, q_scale):
    t = pl.program_id(2)

    @pl.when(t == 0)
    def _():
        s_ref[...] = jnp.zeros_like(s_ref)

    o = _recurrence_tile(q_ref[...] * q_scale, k_ref[...], v_ref[...], lg_ref[...], s_ref)
    o_ref[...] = _gated_head_norm(o, g_ref[...], gn_ref[...]).astype(o_ref.dtype)

    @pl.when(t == pl.num_programs(2) - 1)
    def _():
        so_ref[...] = jnp.transpose(s_ref[...])


def _gla_prompt(proj, logg, g_norm, j, *, batch, seq, heads, dk, dv, lt):
    nt = seq // lt
    kcol = lambda off: pl.BlockSpec((lt, dk), lambda b, h, t: (b * nt + t, off + h))
    vcol = lambda off: pl.BlockSpec((lt, dv), lambda b, h, t: (b * nt + t, off + h))
    v_off = 2 * heads * dk // dv
    return pl.pallas_call(
        functools.partial(_gla_prompt_kernel, q_scale=dk ** -0.5),
        grid=(batch, heads, nt),
        in_specs=[
            kcol(0), kcol(heads), vcol(v_off), vcol(v_off + heads), kcol(0),
            pl.BlockSpec((None, 1, dv), lambda b, h, t: (j, 0, 0)),
        ],
        out_specs=[
            pl.BlockSpec((lt, dv), lambda b, h, t: (b * nt + t, h)),
            pl.BlockSpec((None, None, dk, dv), lambda b, h, t: (b, h, 0, 0)),
        ],
        out_shape=[
            jax.ShapeDtypeStruct((proj.shape[0], heads * dv), bf16),
            jax.ShapeDtypeStruct((batch, heads, dk, dv), f32),
        ],
        scratch_shapes=[pltpu.VMEM((dv, dk), f32)],
        compiler_params=_params("parallel", "parallel", "arbitrary"),
        name="gla_prompt",
    )(proj, proj, proj, proj, logg, g_norm)


def _split3(x):
    hi = x.astype(bf16).astype(f32)
    rest = x - hi
    mid = rest.astype(bf16).astype(f32)
    lo = (rest - mid).astype(bf16).astype(f32)
    return hi, mid, lo


def _state_step(q, k, v, logg, s_ref, so_ref, hh):
    bb, dk = q.shape
    dv = v.shape[1]
    wide = bb * dv
    assert dv & (dv - 1) == 0 and bb & (bb - 1) == 0
    blk_of_lane = lambda rows: lax.broadcasted_iota(jnp.int32, (rows, wide), 1) >> (dv.bit_length() - 1)
    row_of = lambda rows: lax.broadcasted_iota(jnp.int32, (rows, wide), 0)
    row = row_of(4 * bb)
    ones_blk = jnp.where((blk_of_lane(4 * bb) == (row & (bb - 1))) & (row < 3 * bb), 1.0, 0.0).astype(bf16)
    v_rows = jnp.tile(jnp.concatenate([v, jnp.zeros_like(v)], axis=0), (1, bb))
    v_blk = jnp.where(blk_of_lane(2 * bb) == row_of(2 * bb), v_rows, 0.0)
    pad = jnp.zeros((bb, dk), f32)
    over_rows = (((0,), (0,)), ((), ()))
    kv = lax.dot_general(jnp.concatenate([k, pad], axis=0).astype(bf16), v_blk.astype(bf16), over_rows,
                         preferred_element_type=f32)
    e_all = lax.dot_general(jnp.concatenate([*_split3(jnp.exp(logg)), pad], axis=0).astype(bf16), ones_blk,
                            over_rows, preferred_element_type=f32)
    q_all = lax.dot_general(jnp.concatenate([*_split3(q), pad], axis=0).astype(bf16), ones_blk, over_rows,
                            preferred_element_type=f32)
    outs = []
    for r in range(bb):
        c = slice(r * dv, (r + 1) * dv)
        s_new = e_all[:, c] * s_ref[r, hh] + kv[:, c]
        so_ref[r, hh] = s_new
        outs.append(jnp.sum(q_all[:, c] * s_new, axis=0, keepdims=True))
    return jnp.concatenate(outs, axis=0)


def _hgrn_sample_kernel(q_ref, fz_ref, v_ref, g_ref, lb_ref, gn_ref, s_ref, *rest, hb, j, first):
    o_ref, so_ref = rest[-2:]
    if first:
        for layer in range(so_ref.shape[0]):
            if layer != j:
                so_ref[layer] = jnp.zeros(so_ref.shape[1:], f32)
        so_ref = so_ref.at[j]
    dk = HG_DK
    dv = gn_ref.shape[1]
    gn = gn_ref[...]
    for hh in range(hb):
        kc = slice(hh * dk, (hh + 1) * dk)
        vc = slice(hh * dv, (hh + 1) * dv)
        q, k, logf = _hgrn_features(q_ref[:, kc], fz_ref[:, kc], lb_ref[:, kc])
        o = _state_step(q, k, v_ref[:, vc], logf, s_ref, so_ref, hh)
        o_ref[:, vc] = _gated_head_norm(o, g_ref[:, vc], gn)


def _hgrn_sample(proj, lbs, layer, g_norm, j, state, new_states, *, row0, batch, heads, bb, hb):
    dk = HG_DK
    dv = g_norm.shape[-1]
    rb0 = row0 // bb
    hblocks = heads // hb
    n_layers = state.shape[0]
    first = new_states is None
    col = lambda off: pl.BlockSpec((bb, hb * dk), lambda b, h: (rb0 + b, off + h))
    st = pl.BlockSpec((None, bb, hb, dk, dv), lambda b, h: (j, b, h, 0, 0))
    in_specs = [
        col(0), col(hblocks), col(2 * hblocks), col(3 * hblocks),
        pl.BlockSpec((None, 1, hb * dk), lambda b, h: (layer, 0, h)),
        pl.BlockSpec((None, 1, dv), lambda b, h: (j, 0, 0)),
        st,
    ]
    args = [proj, proj, proj, proj, lbs, g_norm, state]
    if first:
        st_out = pl.BlockSpec((n_layers, bb, hb, dk, dv), lambda b, h: (0, b, h, 0, 0))
        aliases = {}
    else:
        st_out = st
        in_specs.append(pl.BlockSpec(memory_space=pl.ANY))
        args.append(new_states)
        aliases = {len(args) - 1: 1}
    return pl.pallas_call(
        functools.partial(_hgrn_sample_kernel, hb=hb, j=j, first=first),
        grid=(batch // bb, hblocks),
        in_specs=in_specs,
        out_specs=[pl.BlockSpec((bb, hb * dv), lambda b, h: (b, h)), st_out],
        out_shape=[
            jax.ShapeDtypeStruct((batch, heads * dv), f32),
            jax.ShapeDtypeStruct(state.shape, f32),
        ],
        input_output_aliases=aliases,
        compiler_params=_params("parallel", "parallel"),
        name="hgrn_sample",
    )(*args)


def _gla_sample_kernel(q_ref, k_ref, v_ref, g_ref, lg_ref, gn_ref, s_ref, o_ref, so_ref, *, q_scale):
    o = _state_step(q_ref[...] * q_scale, k_ref[...], v_ref[...], lg_ref[...], s_ref, so_ref, 0)
    o_ref[...] = _gated_head_norm(o, g_ref[...], gn_ref[...])


def _gla_sample(proj, logg, g_norm, j, state, *, row0, batch, heads, dk, dv, bb):
    rb0 = row0 // bb
    kcol = lambda off: pl.BlockSpec((bb, dk), lambda b, h: (rb0 + b, off + h))
    vcol = lambda off: pl.BlockSpec((bb, dv), lambda b, h: (rb0 + b, off + h))
    v_off = 2 * heads * dk // dv
    return pl.pallas_call(
        functools.partial(_gla_sample_kernel, q_scale=dk ** -0.5),
        grid=(batch // bb, heads),
        in_specs=[
            kcol(0), kcol(heads), vcol(v_off), vcol(v_off + heads), kcol(0),
            pl.BlockSpec((None, 1, dv), lambda b, h: (j, 0, 0)),
            pl.BlockSpec((None, bb, 1, dk, dv), lambda b, h: (j, b, h, 0, 0)),
        ],
        out_specs=[
            pl.BlockSpec((bb, dv), lambda b, h: (b, h)),
            pl.BlockSpec((bb, 1, dk, dv), lambda b, h: (b, h, 0, 0)),
        ],
        out_shape=[
            jax.ShapeDtypeStruct((batch, heads * dv), f32),
            jax.ShapeDtypeStruct((batch, heads, dk, dv), f32),
        ],
        compiler_params=_params("parallel", "parallel"),
        name="gla_sample",
    )(proj, proj, proj, proj, logg, g_norm, state)


def _gla_gate_kernel(u_ref, w1_ref, w2_ref, b_ref, o_ref):
    low = jnp.dot(u_ref[...], w1_ref[...].astype(bf16), preferred_element_type=f32)
    gk = jnp.dot(low.astype(bf16), w2_ref[...].astype(bf16), preferred_element_type=f32) + b_ref[...]
    log_sig = jnp.minimum(gk, 0.0) - jnp.log1p(jnp.exp(-jnp.abs(gk)))
    o_ref[...] = log_sig / GLA_GATE_NORMALIZER


def _gla_gate(u, w1, w2, bias, *, tm):
    m, k = u.shape
    r = w1.shape[1]
    n = w2.shape[1]
    full = lambda shape: pl.BlockSpec(shape, lambda i: (0, 0))
    return pl.pallas_call(
        _gla_gate_kernel,
        grid=(m // tm,),
        in_specs=[pl.BlockSpec((tm, k), lambda i: (i, 0)), full((k, r)), full((r, n)), full((1, n))],
        out_specs=pl.BlockSpec((tm, n), lambda i: (i, 0)),
        out_shape=jax.ShapeDtypeStruct((m, n), f32),
        compiler_params=_params("parallel"),
        name="gla_gate",
    )(u, w1, w2, bias)


def _pool_prompt_kernel(u_ref, w_ref, sc_ref, o_ref):
    g = pl.program_id(1)
    x = u_ref[...]
    row = lax.broadcasted_iota(jnp.int32, x.shape, 0)
    pos = lax.broadcasted_iota(jnp.int32, (x.shape[0], 1), 0) + 1
    for gi, win in enumerate(POOL_WINDOWS):
        @pl.when(g == gi)
        def _(win=win):
            s = x
            sh = 1
            while sh < win:
                s = s + jnp.where(row >= sh, pltpu.roll(s, sh, axis=0), 0.0)
                sh *= 2
            cnt = jnp.minimum(pos, win).astype(f32)
            d = s / cnt - x
            y = jnp.dot(d.astype(bf16), w_ref[...].astype(bf16), preferred_element_type=f32)
            o_ref[...] = y * sc_ref[...]


def _pool_prompt(u32, w, scale, j, *, batch, seq):
    groups, gc = w.shape[1], w.shape[2]
    return pl.pallas_call(
        _pool_prompt_kernel,
        grid=(batch, groups),
        in_specs=[
            pl.BlockSpec((seq, gc), lambda b, g: (b, g)),
            pl.BlockSpec((None, None, gc, gc), lambda b, g: (j, g, 0, 0)),
            pl.BlockSpec((None, 1, gc), lambda b, g: (j, 0, g)),
        ],
        out_specs=pl.BlockSpec((seq, gc), lambda b, g: (b, g)),
        out_shape=jax.ShapeDtypeStruct((u32.shape[0], groups * gc), f32),
        compiler_params=_params("parallel", "parallel"),
        name="pool_prompt",
    )(u32, w, scale)


def _pool_sample_kernel(u_ref, buf_ref, w_ref, sc_ref, o_ref):
    g = pl.program_id(0)
    x = u_ref[...]
    nbuf = buf_ref.shape[1]
    for gi, win in enumerate(POOL_WINDOWS):
        @pl.when(g == gi)
        def _(win=win):
            s = x
            for r in range(nbuf - (win - 1), nbuf):
                s = s + buf_ref[:, r, :]
            d = s / float(win) - x
            y = jnp.dot(d.astype(bf16), w_ref[...].astype(bf16), preferred_element_type=f32)
            o_ref[...] = y * sc_ref[...]


def _pool_sample(u32, buf, w, scale, j, *, row0, batch):
    groups, gc = w.shape[1], w.shape[2]
    nbuf = buf.shape[2]
    return pl.pallas_call(
        _pool_sample_kernel,
        grid=(groups,),
        in_specs=[
            pl.BlockSpec((batch, gc), lambda g: (row0 // batch, g)),
            pl.BlockSpec((None, batch, nbuf, gc), lambda g: (j, 0, 0, g)),
            pl.BlockSpec((None, None, gc, gc), lambda g: (j, g, 0, 0)),
            pl.BlockSpec((None, 1, gc), lambda g: (j, 0, g)),
        ],
        out_specs=pl.BlockSpec((batch, gc), lambda g: (0, g)),
        out_shape=jax.ShapeDtypeStruct((batch, groups * gc), f32),
        compiler_params=_params("parallel"),
        name="pool_sample",
    )(u32, buf, w, scale)


def _xattn_prompt_kernel(q_ref, k_ref, v_ref, o_ref, *, heads):
    dh = q_ref.shape[1] // heads
    scale = dh ** -0.5
    for h in range(heads):
        c = slice(h * dh, (h + 1) * dh)
        s = lax.dot_general(q_ref[:, c], k_ref[:, c].astype(bf16), (((1,), (1,)), ((), ())),
                            preferred_element_type=f32) * scale
        p = jnp.exp(s - jnp.max(s, axis=-1, keepdims=True))
        p = p / jnp.sum(p, axis=-1, keepdims=True)
        o = jnp.dot(p.astype(bf16), v_ref[:, c].astype(bf16), preferred_element_type=f32)
        o_ref[:, c] = o.astype(o_ref.dtype)


def _xattn_prompt(q, mem_k, mem_v, *, batch, seq, n_mem, tq):
    d = q.shape[1]
    nq = seq // tq
    kv = pl.BlockSpec((n_mem, d), lambda b, i: (b, 0))
    return pl.pallas_call(
        functools.partial(_xattn_prompt_kernel, heads=XA_HEADS),
        grid=(batch, nq),
        in_specs=[pl.BlockSpec((tq, d), lambda b, i: (b * nq + i, 0)), kv, kv],
        out_specs=pl.BlockSpec((tq, d), lambda b, i: (b * nq + i, 0)),
        out_shape=jax.ShapeDtypeStruct((q.shape[0], d), bf16),
        compiler_params=_params("parallel", "parallel"),
        name="xattn_prompt",
    )(q, mem_k, mem_v)


def _xattn_sample_kernel(q_ref, k_ref, v_ref, o_ref, *, bb):
    n_mem, heads, dh = k_ref.shape[1:]
    rows = n_mem * heads
    scale = dh ** -0.5
    qrows = q_ref.shape[1]
    col_head = lax.broadcasted_iota(jnp.int32, (qrows, rows), 1) & (heads - 1)
    row_head = lax.broadcasted_iota(jnp.int32, (qrows, rows), 0) & (heads - 1)
    own = col_head == row_head
    for r in range(bb):
        k2 = k_ref[r].reshape(rows, dh).astype(bf16)
        v2 = v_ref[r].reshape(rows, dh).astype(bf16)
        s = lax.dot_general(q_ref[r], k2, (((1,), (1,)), ((), ())), preferred_element_type=f32) * scale
        s = jnp.where(own, s, -jnp.inf)
        p = jnp.exp(s - jnp.max(s, axis=-1, keepdims=True))
        p = p / jnp.sum(p, axis=-1, keepdims=True)
        o_ref[r] = jnp.dot(p.astype(bf16), v2, preferred_element_type=f32)


def _xattn_sample(q, cache_k, cache_v, layer, *, bb):
    batch, qrows, dh = q.shape
    n_mem, heads = cache_k.shape[2], cache_k.shape[3]
    assert heads & (heads - 1) == 0 and qrows % heads == 0
    kv = pl.BlockSpec((None, bb, n_mem, heads, dh), lambda b: (layer, b, 0, 0, 0))
    qo = pl.BlockSpec((bb, qrows, dh), lambda b: (b, 0, 0))
    return pl.pallas_call(
        functools.partial(_xattn_sample_kernel, bb=bb),
        grid=(batch // bb,),
        in_specs=[qo, kv, kv],
        out_specs=qo,
        out_shape=jax.ShapeDtypeStruct((batch, qrows, dh), f32),
        compiler_params=_params("parallel"),
        name="xattn_sample",
    )(q, cache_k, cache_v)


def kernel(x_prompt, x_sample, state_hgrn, state_gla, state_pool, cache_mem_k, cache_mem_v, mem_prompt, norm_gains, hgrn_w_in, hgrn_lb, hgrn_g_norm, hgrn_w_o, gla_w_in, gla_w_gk1, gla_w_gk2, gla_b_gk, gla_g_norm, gla_w_o, pool_w, pool_scale, mem_norm, xa_w_q, xa_w_k, xa_w_v, xa_w_o, mlp_w_up, mlp_w_down):
    batch, seq, d = x_prompt.shape
    dec_batch = x_sample.shape[0]
    depth = norm_gains.shape[0]
    n_mem = mem_prompt.shape[1]
    hg_heads = state_hgrn.shape[2]
    gla_dk, gla_dv = state_gla.shape[3], state_gla.shape[4]
    n_prompt = batch * seq
    m_rows = n_prompt + dec_batch
    tm = m_rows // TOKEN_TILES_WS
    tr = m_rows // TOKEN_TILES_RES
    assert tm * TOKEN_TILES_WS == m_rows and tm % 16 == 0
    assert tr * TOKEN_TILES_RES == m_rows and tr % 16 == 0
    dh = d // XA_HEADS

    x = jnp.concatenate([x_prompt.reshape(n_prompt, d), x_sample.reshape(dec_batch, d)], axis=0)
    mem = mem_prompt.reshape(batch * n_mem, d)
    gain = lambda i, n: norm_gains[i, n][None, :]
    with_sample = lambda full, rows: lax.dynamic_update_slice(full, rows.astype(full.dtype), (n_prompt, 0))

    lbs = _lower_bounds(hgrn_lb)[:, None, :]
    hgrn_g_norm = hgrn_g_norm[:, None, :]
    gla_g_norm = gla_g_norm[:, None, :]
    pool_scale = pool_scale[:, None, :]
    hgrn_w_o_b = _cast_bf16(hgrn_w_o)
    gla_w_o_b = _cast_bf16(gla_w_o)
    xa_w_o_b = _cast_bf16(xa_w_o)
    mlp_w_down_b = _cast_bf16(mlp_w_down)
    rank = gla_w_gk1.shape[2]
    gk1 = jnp.pad(gla_w_gk1, ((0, 0), (0, 0), (0, 128 - rank)))
    gk2 = jnp.pad(gla_w_gk2, ((0, 0), (0, 128 - rank), (0, 0)))

    new_h_p, new_g_p, new_g_s, new_p_p, new_p_s, mem_ks, mem_vs = [], [], [], [], [], [], []
    new_h_s = None
    u = _rmsnorm(x, gain(0, 0), bf16, tr)
    for i in range(depth):
        j, kind = i // 3, i % 3
        if kind == 0:
            proj = _mm_ws(u, hgrn_w_in, j, tm=tm, tn=1024)
            o, h_p = _hgrn_prompt(proj, lbs, i, hgrn_g_norm, j, batch=batch, seq=seq, heads=hg_heads, lt=512)
            o_s, new_h_s = _hgrn_sample(proj, lbs, i, hgrn_g_norm, j, state_hgrn, new_h_s, row0=n_prompt,
                                        batch=dec_batch, heads=hg_heads, bb=8, hb=4)
            new_h_p.append(h_p)
            x, u = _mm_res(with_sample(o, o_s), hgrn_w_o_b, j, x, gain(i, 1), gain(i, 2), tm=tr, tk=d)
        elif kind == 1:
            proj = _mm_ws(u, gla_w_in, j, tm=tm, tn=1024)
            logg = _gla_gate(u, gk1[j], gk2[j], gla_b_gk[j][None, :], tm=tm)
            o, g_p = _gla_prompt(proj, logg, gla_g_norm, j, batch=batch, seq=seq, heads=GLA_HEADS,
                                 dk=gla_dk, dv=gla_dv, lt=256)
            o_s, g_s = _gla_sample(proj, logg, gla_g_norm, j, state_gla, row0=n_prompt, batch=dec_batch,
                                   heads=GLA_HEADS, dk=gla_dk, dv=gla_dv, bb=8)
            new_g_p.append(g_p)
            new_g_s.append(g_s)
            x, u = _mm_res(with_sample(o, o_s), gla_w_o_b, j, x, gain(i, 1), gain(i, 2), tm=tr, tk=d)
        else:
            u32 = _rmsnorm(x, gain(i, 0), f32, tr)
            m = _pool_prompt(u32, pool_w, pool_scale, j, batch=batch, seq=seq)
            m_s = _pool_sample(u32, state_pool, pool_w, pool_scale, j, row0=n_prompt, batch=dec_batch)
            nbuf = state_pool.shape[2]
            new_p_p.append(jnp.stack([u32[(b + 1) * seq - nbuf:(b + 1) * seq] for b in range(batch)]))
            new_p_s.append(jnp.concatenate([state_pool[j][:, 1:], u32[n_prompt:][:, None, :]], axis=1))
            x, u = _add_norm(x, with_sample(m, m_s), gain(i, 1), gain(i, 2), tr)

        q = _mm_ws(u, xa_w_q, i, tm=tm, tn=1024, out_dtype=bf16)
        mem_n = _rmsnorm(mem, mem_norm[i][None, :], bf16, 512)
        mem_k = _mm_ws(mem_n, xa_w_k, i, tm=512, tn=1024)
        mem_v = _mm_ws(mem_n, xa_w_v, i, tm=512, tn=1024)
        mem_ks.append(mem_k)
        mem_vs.append(mem_v)
        a = _xattn_prompt(q, mem_k, mem_v, batch=batch, seq=seq, n_mem=n_mem, tq=512)
        q_s = jnp.tile(q[n_prompt:].reshape(dec_batch, XA_HEADS, dh), (1, BF16_ROWS // XA_HEADS, 1))
        a_s = _xattn_sample(q_s, cache_mem_k, cache_mem_v, i, bb=4)[:, :XA_HEADS]
        x, u = _mm_res(with_sample(a, a_s.reshape(dec_batch, d)), xa_w_o_b, i, x, gain(i, 3), gain(i, 4),
                       tm=tr, tk=d)

        h = _mm_ws(u, mlp_w_up, i, tm=tm, tn=1024, out_dtype=bf16, act="relu2")
        g_next = gain(i + 1, 0) if i + 1 < depth else None
        x, u = _mm_res(h, mlp_w_down_b, i, x, gain(i, 5), g_next, tm=tr, tk=1024)

    kv_shape = (depth, batch, n_mem, XA_HEADS, dh)
    return (
        x[:n_prompt].reshape(batch, seq, d),
        x[n_prompt:].reshape(dec_batch, 1, d),
        jnp.stack(new_h_p),
        new_h_s,
        jnp.stack(new_g_p),
        jnp.stack(new_g_s),
        jnp.stack(new_p_p),
        jnp.stack(new_p_s),
        jnp.stack(mem_ks).reshape(kv_shape),
        jnp.stack(mem_vs).reshape(kv_shape),
    )
```

```python
import functools

import jax
import jax.numpy as jnp
from jax import lax
from jax.experimental import pallas as pl
from jax.experimental.pallas import tpu as pltpu

f32 = jnp.float32
bf16 = jnp.bfloat16

EPS = 1e-6
LOG2E = 1.4426950408889634
POOL_WINDOWS = (2, 4, 8, 16)
GLA_GATE_NORMALIZER = 16.0
XA_HEADS = 4
HG_DK = 128
GLA_HEADS = 4
CHUNK = 32
LANES = 128
SUBLANES = 8
BF16_ROWS = 16
TOKEN_TILES_WS = 5
TOKEN_TILES_RES = 13
VMEM_LIMIT = 56 * 1024 * 1024


def _params(*sem, vmem=VMEM_LIMIT):
    return pltpu.CompilerParams(dimension_semantics=sem, vmem_limit_bytes=vmem)


def _rms_scale(x):
    return lax.rsqrt(jnp.mean(x * x, axis=-1, keepdims=True) + EPS)


def _silu(x):
    return x / (1.0 + jnp.exp(-x))


def _rmsnorm_kernel(x_ref, g_ref, o_ref):
    x = x_ref[...]
    o_ref[...] = (x * _rms_scale(x) * g_ref[...]).astype(o_ref.dtype)


def _rmsnorm(x, g, out_dtype, tm):
    m, d = x.shape
    return pl.pallas_call(
        _rmsnorm_kernel,
        grid=(m // tm,),
        in_specs=[pl.BlockSpec((tm, d), lambda i: (i, 0)), pl.BlockSpec((1, d), lambda i: (0, 0))],
        out_specs=pl.BlockSpec((tm, d), lambda i: (i, 0)),
        out_shape=jax.ShapeDtypeStruct((m, d), out_dtype),
        compiler_params=_params("parallel"),
        name="rmsnorm",
    )(x, g)


def _add_norm_kernel(x_ref, m_ref, gp_ref, gn_ref, xo_ref, uo_ref):
    m = m_ref[...]
    xn = x_ref[...] + m * _rms_scale(m) * gp_ref[...]
    xo_ref[...] = xn
    uo_ref[...] = (xn * _rms_scale(xn) * gn_ref[...]).astype(uo_ref.dtype)


def _add_norm(x, m, g_post, g_next, tm):
    rows, d = x.shape
    row = pl.BlockSpec((tm, d), lambda i: (i, 0))
    gain = pl.BlockSpec((1, d), lambda i: (0, 0))
    return pl.pallas_call(
        _add_norm_kernel,
        grid=(rows // tm,),
        in_specs=[row, row, gain, gain],
        out_specs=[row, row],
        out_shape=[jax.ShapeDtypeStruct((rows, d), f32), jax.ShapeDtypeStruct((rows, d), bf16)],
        compiler_params=_params("parallel"),
        name="add_norm",
    )(x, m, g_post, g_next)


def _cast_kernel(x_ref, o_ref):
    o_ref[...] = x_ref[...].astype(o_ref.dtype)


def _cast_bf16(w, tr=512):
    nl, k, n = w.shape
    spec = pl.BlockSpec((None, tr, n), lambda l, i: (l, i, 0))
    return pl.pallas_call(
        _cast_kernel,
        grid=(nl, k // tr),
        in_specs=[spec],
        out_specs=spec,
        out_shape=jax.ShapeDtypeStruct(w.shape, bf16),
        compiler_params=_params("parallel", "parallel"),
        name="cast_bf16",
    )(w)


def _lower_bounds_kernel(l_ref, o_ref):
    z = l_ref[...]
    p = jnp.exp(z - jnp.max(z, axis=0, keepdims=True))
    p = p / jnp.sum(p, axis=0, keepdims=True)
    acc = jnp.zeros_like(p[0:1])
    for i in range(z.shape[0]):
        acc = acc + p[i:i + 1]
        o_ref[i:i + 1, :] = acc - p[0:1]


def _lower_bounds(lb_logits):
    return pl.pallas_call(
        _lower_bounds_kernel,
        out_shape=jax.ShapeDtypeStruct(lb_logits.shape, f32),
        name="hgrn_lower_bounds",
    )(lb_logits)


def _mm_ws_kernel(x_ref, w_ref, o_ref, wbf_ref, *, act):
    @pl.when(pl.program_id(1) == 0)
    def _():
        wbf_ref[...] = w_ref[...].astype(bf16)

    acc = jnp.dot(x_ref[...], wbf_ref[...], preferred_element_type=f32)
    if act == "relu2":
        acc = jnp.square(jnp.maximum(acc, 0.0))
    o_ref[...] = acc.astype(o_ref.dtype)


def _mm_ws(x, w, layer, *, tm, tn, out_dtype=f32, act=None):
    m, k = x.shape
    n = w.shape[2]
    return pl.pallas_call(
        functools.partial(_mm_ws_kernel, act=act),
        grid=(n // tn, m // tm),
        in_specs=[
            pl.BlockSpec((tm, k), lambda j, i: (i, 0)),
            pl.BlockSpec((None, k, tn), lambda j, i: (layer, 0, j)),
        ],
        out_specs=pl.BlockSpec((tm, tn), lambda j, i: (i, j)),
        out_shape=jax.ShapeDtypeStruct((m, n), out_dtype),
        scratch_shapes=[pltpu.VMEM((k, tn), bf16)],
        compiler_params=_params("arbitrary", "arbitrary"),
        name="matmul_ws",
    )(x, w)


def _mm_res_kernel(a_ref, w_ref, x_ref, gp_ref, *rest, nk, with_next):
    rest = list(rest)
    gn_ref = rest.pop(0) if with_next else None
    xo_ref = rest.pop(0)
    uo_ref = rest.pop(0) if with_next else None

    def finish(m, rows=slice(None)):
        xn = x_ref[rows, :] + m * _rms_scale(m) * gp_ref[...]
        xo_ref[rows, :] = xn
        if with_next:
            uo_ref[rows, :] = (xn * _rms_scale(xn) * gn_ref[...]).astype(uo_ref.dtype)

    if nk == 1:
        half = a_ref.shape[0] // 2
        for rows in (slice(0, half), slice(half, 2 * half)):
            finish(jnp.dot(a_ref[rows, :], w_ref[...], preferred_element_type=f32), rows)
        return
    acc_ref = rest.pop(0)
    kk = pl.program_id(1)

    @pl.when(kk == 0)
    def _():
        acc_ref[...] = jnp.zeros_like(acc_ref)

    acc_ref[...] += jnp.dot(a_ref[...], w_ref[...], preferred_element_type=f32)

    @pl.when(kk == nk - 1)
    def _():
        finish(acc_ref[...])


def _mm_res(a, w, layer, x, g_post, g_next, *, tm, tk):
    m, k = a.shape
    d = w.shape[2]
    nk = k // tk
    with_next = g_next is not None
    row = pl.BlockSpec((tm, d), lambda i, kk: (i, 0))
    gain = pl.BlockSpec((1, d), lambda i, kk: (0, 0))
    in_specs = [
        pl.BlockSpec((tm, tk), lambda i, kk: (i, kk)),
        pl.BlockSpec((None, tk, d), lambda i, kk: (layer, kk, 0)),
        row,
        gain,
    ]
    args = [a, w, x, g_post]
    out_specs = [row]
    out_shape = [jax.ShapeDtypeStruct((m, d), f32)]
    if with_next:
        in_specs.append(gain)
        args.append(g_next)
        out_specs.append(row)
        out_shape.append(jax.ShapeDtypeStruct((m, d), bf16))
    res = pl.pallas_call(
        functools.partial(_mm_res_kernel, nk=nk, with_next=with_next),
        grid=(m // tm, nk),
        in_specs=in_specs,
        out_specs=out_specs,
        out_shape=out_shape,
        scratch_shapes=[pltpu.VMEM((tm, d), f32)] if nk > 1 else [],
        compiler_params=_params("parallel", "arbitrary"),
        name="matmul_residual_norm",
    )(*args)
    return (res[0], res[1]) if with_next else (res[0], None)


def _segmented_cumsum(x, seg):
    pos = lax.broadcasted_iota(jnp.int32, x.shape, 0) & (seg - 1)
    sh = 1
    while sh < seg:
        x = x + jnp.where(pos >= sh, pltpu.roll(x, sh, axis=0), 0.0)
        sh *= 2
    return x


def _recurrence_tile(q, k, v, logg, st_ref):
    r, dk = q.shape
    dv = v.shape[1]
    c, g8 = CHUNK, SUBLANES
    nc, ng = r // c, c // g8
    n8 = nc * ng
    b = _segmented_cumsum(logg * LOG2E, c)
    q3, k3, b3 = (x.reshape(nc, c, dk) for x in (q, k, b))
    v3 = v.astype(bf16).reshape(nc, c, dv)
    last = b3[:, c - 1:c, :]
    qe = (q3 * jnp.exp2(b3)).astype(bf16)
    ke = (k3 * jnp.exp2(last - b3)).astype(bf16)

    qg, kg, bg = (x.reshape(n8, g8, dk) for x in (q, k, b))
    lane = lax.broadcasted_iota(jnp.int32, (n8, g8, c), 2)
    group = lax.broadcasted_iota(jnp.int32, (n8, g8, c), 0) & (ng - 1)
    rel = lane - group * g8
    row = lax.broadcasted_iota(jnp.int32, (n8, g8, c), 1)
    diag = jnp.zeros((n8, g8, c), f32)
    for j in range(g8):
        w = jnp.exp2(bg - bg[:, j:j + 1, :])
        col = jnp.sum(qg * w * kg[:, j:j + 1, :], axis=-1, keepdims=True)
        diag = jnp.where(rel == j, col, diag)
    diag = jnp.where(rel <= row, diag, 0.0)

    srow = lax.broadcasted_iota(jnp.int32, (nc, c, dk), 1)
    blocks = [jnp.zeros((nc, g8, c), f32)]
    for i in range(1, ng):
        lo = i * g8
        anchor = b3[:, lo:lo + 1, :]
        qi = (q3[:, lo:lo + g8, :] * jnp.exp2(b3[:, lo:lo + g8, :] - anchor)).astype(bf16)
        ki = jnp.where(srow < lo, k3 * jnp.exp2(anchor - b3), 0.0).astype(bf16)
        blocks.append(jnp.einsum("ctk,csk->cts", qi, ki, preferred_element_type=f32))
    a = jnp.concatenate(blocks, axis=1) + diag.reshape(nc, c, c)

    o_intra = jnp.einsum("cts,csv->ctv", a.astype(bf16), v3, preferred_element_type=f32)
    kvt = jnp.einsum("csv,csk->cvk", v3, ke, preferred_element_type=f32)
    st = st_ref[...]
    o_inter = []
    for ci in range(nc):
        o_inter.append(lax.dot_general(qe[ci], st.astype(bf16), (((1,), (1,)), ((), ())),
                                       preferred_element_type=f32))
        st = st * jnp.exp2(last[ci]) + kvt[ci]
    st_ref[...] = st
    return (o_intra + jnp.stack(o_inter)).reshape(r, dv)


def _gated_head_norm(o, gate, gn):
    return o * _rms_scale(o) * gn * _silu(gate)


def _hgrn_features(q_raw, fz, lb):
    e = jnp.exp(-jnp.abs(fz))
    r = 1.0 / (1.0 + e)
    nonneg = fz >= 0
    sig = jnp.where(nonneg, r, e * r)
    sig_neg = jnp.where(nonneg, e * r, r)
    log_sig = jnp.minimum(fz, 0.0) - jnp.log(1.0 + e)
    logf = jnp.where(lb > 0, jnp.log(lb + (1.0 - lb) * sig), log_sig)
    k = (1.0 - lb) * sig_neg
    return _silu(q_raw), k, logf


def _sequence_step(o_ref, so_ref, s_ref, *, nt, steps, compute):
    step = pl.program_id(1)
    t = lax.rem(step, nt)

    @pl.when(step == steps)
    def _():
        o_ref[...] = jnp.zeros_like(o_ref)

    @pl.when(step < steps)
    def _():
        @pl.when(t == 0)
        def _():
            s_ref[...] = jnp.zeros_like(s_ref)

        o_ref[...] = compute().astype(o_ref.dtype)

        @pl.when(t == nt - 1)
        def _():
            so_ref[...] = jnp.transpose(s_ref[...])


def _sequence_call(kernel, in_blocks, args, gains, *, name, rows, batch, seq, heads, dk, dv, lt):
    nt = seq // lt
    steps = batch * nt
    assert nt * lt == seq and 0 < rows - batch * seq <= lt
    tile = lambda s: jnp.minimum(s, steps - 1)
    tok = lambda width, off: pl.BlockSpec((lt, width), lambda h, s: (tile(s), off + h))

    def small(idx, width, per_head):
        return pl.BlockSpec((None, 1, width), lambda h, s: (idx, 0, h if per_head else 0))

    return pl.pallas_call(
        functools.partial(kernel, nt=nt, steps=steps),
        grid=(heads, steps + 1),
        in_specs=[tok(w, off) for w, off in in_blocks] + [small(i, w, p) for _, i, w, p in gains],
        out_specs=[
            pl.BlockSpec((lt, dv), lambda h, s: (s, h)),
            pl.BlockSpec((None, None, dk, dv), lambda h, s: (tile(s) // nt, h, 0, 0)),
        ],
        out_shape=[
            jax.ShapeDtypeStruct((rows, heads * dv), bf16),
            jax.ShapeDtypeStruct((batch, heads, dk, dv), f32),
        ],
        scratch_shapes=[pltpu.VMEM((dv, dk), f32)],
        compiler_params=_params("parallel", "arbitrary"),
        name=name,
    )(*args, *[g[0] for g in gains])


def _hgrn_prompt_kernel(q_ref, fz_ref, v_ref, g_ref, lb_ref, gn_ref, o_ref, so_ref, s_ref, *, nt, steps):
    def compute():
        q, k, logf = _hgrn_features(q_ref[...], fz_ref[...], lb_ref[...])
        o = _recurrence_tile(q, k, v_ref[...], logf, s_ref)
        return _gated_head_norm(o, g_ref[...], gn_ref[...])

    _sequence_step(o_ref, so_ref, s_ref, nt=nt, steps=steps, compute=compute)


def _hgrn_prompt(proj, lbs, layer, g_norm, j, *, batch, seq, heads, lt):
    dk = HG_DK
    dv = g_norm.shape[-1]
    return _sequence_call(
        _hgrn_prompt_kernel, [(dk, 0), (dk, heads), (dk, 2 * heads), (dk, 3 * heads)], [proj] * 4,
        [(lbs, layer, dk, True), (g_norm, j, dv, False)],
        name="hgrn_prompt", rows=proj.shape[0], batch=batch, seq=seq, heads=heads, dk=dk, dv=dv, lt=lt)


def _gla_prompt_kernel(q_ref, k_ref, v_ref, g_ref, lg_ref, gn_ref, o_ref, so_ref, s_ref, *, nt, steps, q_scale):
    def compute():
        o = _recurrence_tile(q_ref[...] * q_scale, k_ref[...], v_ref[...], lg_ref[...], s_ref)
        return _gated_head_norm(o, g_ref[...], gn_ref[...])

    _sequence_step(o_ref, so_ref, s_ref, nt=nt, steps=steps, compute=compute)


def _gla_prompt(proj, logg, g_norm, j, *, batch, seq, heads, dk, dv, lt):
    v_off = 2 * heads * dk // dv
    return _sequence_call(
        functools.partial(_gla_prompt_kernel, q_scale=dk ** -0.5),
        [(dk, 0), (dk, heads), (dv, v_off), (dv, v_off + heads), (dk, 0)], [proj] * 4 + [logg],
        [(g_norm, j, dv, False)],
        name="gla_prompt", rows=proj.shape[0], batch=batch, seq=seq, heads=heads, dk=dk, dv=dv, lt=lt)


def _split3(x):
    hi = x.astype(bf16).astype(f32)
    rest = x - hi
    mid = rest.astype(bf16).astype(f32)
    lo = (rest - mid).astype(bf16).astype(f32)
    return hi, mid, lo


def _state_step(q, k, v, logg, s_ref, so_ref, hh):
    bb, dk = q.shape
    dv = v.shape[1]
    assert dv % LANES == 0 and dv & (dv - 1) == 0 and bb & (bb - 1) == 0

    def blocks(rows, width):
        shape = (rows, bb * width)
        return (lax.broadcasted_iota(jnp.int32, shape, 0),
                lax.broadcasted_iota(jnp.int32, shape, 1) >> (width.bit_length() - 1))

    row, blk = blocks(4 * bb, LANES)
    ones_blk = jnp.where((blk == (row & (bb - 1))) & (row < 3 * bb), 1.0, 0.0).astype(bf16)
    row, blk = blocks(2 * bb, dv)
    v_blk = jnp.where(blk == row, jnp.tile(jnp.concatenate([v, jnp.zeros_like(v)], axis=0), (1, bb)), 0.0)
    pad = jnp.zeros((bb, dk), f32)
    over_rows = (((0,), (0,)), ((), ()))
    kv = lax.dot_general(jnp.concatenate([k, pad], axis=0).astype(bf16), v_blk.astype(bf16), over_rows,
                         preferred_element_type=f32)
    e_all = lax.dot_general(jnp.concatenate([*_split3(jnp.exp(logg)), pad], axis=0).astype(bf16), ones_blk,
                            over_rows, preferred_element_type=f32)
    q_all = lax.dot_general(jnp.concatenate([*_split3(q), pad], axis=0).astype(bf16), ones_blk, over_rows,
                            preferred_element_type=f32)
    outs = []
    for r in range(bb):
        e_r = e_all[:, r * LANES:(r + 1) * LANES]
        q_r = q_all[:, r * LANES:(r + 1) * LANES]
        parts = []
        for c0 in range(0, dv, LANES):
            s_new = e_r * s_ref[r, hh, :, c0:c0 + LANES] + kv[:, r * dv + c0:r * dv + c0 + LANES]
            so_ref[r, hh, :, c0:c0 + LANES] = s_new
            parts.append(jnp.sum(q_r * s_new, axis=0, keepdims=True))
        outs.append(jnp.concatenate(parts, axis=1))
    return jnp.concatenate(outs, axis=0)


def _hgrn_sample_kernel(q_ref, fz_ref, v_ref, g_ref, lb_ref, gn_ref, s_ref, *rest, hb, j, first):
    o_ref, so_ref = rest[-2:]
    if first:
        for layer in range(so_ref.shape[0]):
            if layer != j:
                so_ref[layer] = jnp.zeros(so_ref.shape[1:], f32)
        so_ref = so_ref.at[j]
    dk = HG_DK
    dv = gn_ref.shape[1]
    gn = gn_ref[...]
    for hh in range(hb):
        kc = slice(hh * dk, (hh + 1) * dk)
        vc = slice(hh * dv, (hh + 1) * dv)
        q, k, logf = _hgrn_features(q_ref[:, kc], fz_ref[:, kc], lb_ref[:, kc])
        o = _state_step(q, k, v_ref[:, vc], logf, s_ref, so_ref, hh)
        o_ref[:, vc] = _gated_head_norm(o, g_ref[:, vc], gn)


def _hgrn_sample(proj, lbs, layer, g_norm, j, state, new_states, *, row0, batch, heads, bb, hb):
    dk = HG_DK
    dv = g_norm.shape[-1]
    rb0 = row0 // bb
    hblocks = heads // hb
    n_layers = state.shape[0]
    first = new_states is None
    col = lambda off: pl.BlockSpec((bb, hb * dk), lambda b, h: (rb0 + b, off + h))
    st = pl.BlockSpec((None, bb, hb, dk, dv), lambda b, h: (j, b, h, 0, 0))
    in_specs = [
        col(0), col(hblocks), col(2 * hblocks), col(3 * hblocks),
        pl.BlockSpec((None, 1, hb * dk), lambda b, h: (layer, 0, h)),
        pl.BlockSpec((None, 1, dv), lambda b, h: (j, 0, 0)),
        st,
    ]
    args = [proj, proj, proj, proj, lbs, g_norm, state]
    if first:
        st_out = pl.BlockSpec((n_layers, bb, hb, dk, dv), lambda b, h: (0, b, h, 0, 0))
        aliases = {}
    else:
        st_out = st
        in_specs.append(pl.BlockSpec(memory_space=pl.ANY))
        args.append(new_states)
        aliases = {len(args) - 1: 1}
    return pl.pallas_call(
        functools.partial(_hgrn_sample_kernel, hb=hb, j=j, first=first),
        grid=(batch // bb, hblocks),
        in_specs=in_specs,
        out_specs=[pl.BlockSpec((bb, hb * dv), lambda b, h: (b, h)), st_out],
        out_shape=[
            jax.ShapeDtypeStruct((batch, heads * dv), f32),
            jax.ShapeDtypeStruct(state.shape, f32),
        ],
        input_output_aliases=aliases,
        compiler_params=_params("parallel", "parallel"),
        name="hgrn_sample",
    )(*args)


def _gla_sample_kernel(q_ref, k_ref, v_ref, g_ref, lg_ref, gn_ref, s_ref, o_ref, so_ref, *, q_scale):
    o = _state_step(q_ref[...] * q_scale, k_ref[...], v_ref[...], lg_ref[...], s_ref, so_ref, 0)
    o_ref[...] = _gated_head_norm(o, g_ref[...], gn_ref[...])


def _gla_sample(proj, logg, g_norm, j, state, *, row0, batch, heads, dk, dv, bb):
    rb0 = row0 // bb
    kcol = lambda off: pl.BlockSpec((bb, dk), lambda b, h: (rb0 + b, off + h))
    vcol = lambda off: pl.BlockSpec((bb, dv), lambda b, h: (rb0 + b, off + h))
    v_off = 2 * heads * dk // dv
    return pl.pallas_call(
        functools.partial(_gla_sample_kernel, q_scale=dk ** -0.5),
        grid=(batch // bb, heads),
        in_specs=[
            kcol(0), kcol(heads), vcol(v_off), vcol(v_off + heads), kcol(0),
            pl.BlockSpec((None, 1, dv), lambda b, h: (j, 0, 0)),
            pl.BlockSpec((None, bb, 1, dk, dv), lambda b, h: (j, b, h, 0, 0)),
        ],
        out_specs=[
            pl.BlockSpec((bb, dv), lambda b, h: (b, h)),
            pl.BlockSpec((bb, 1, dk, dv), lambda b, h: (b, h, 0, 0)),
        ],
        out_shape=[
            jax.ShapeDtypeStruct((batch, heads * dv), f32),
            jax.ShapeDtypeStruct((batch, heads, dk, dv), f32),
        ],
        compiler_params=_params("parallel", "parallel"),
        name="gla_sample",
    )(proj, proj, proj, proj, logg, g_norm, state)


def _gla_gate_kernel(u_ref, w1_ref, w2_ref, b_ref, o_ref):
    low = jnp.dot(u_ref[...], w1_ref[...].astype(bf16), preferred_element_type=f32)
    gk = jnp.dot(low.astype(bf16), w2_ref[...].astype(bf16), preferred_element_type=f32) + b_ref[...]
    log_sig = jnp.minimum(gk, 0.0) - jnp.log1p(jnp.exp(-jnp.abs(gk)))
    o_ref[...] = log_sig / GLA_GATE_NORMALIZER


def _gla_gate(u, w1, w2, bias, *, tm):
    m, k = u.shape
    r = w1.shape[1]
    n = w2.shape[1]
    full = lambda shape: pl.BlockSpec(shape, lambda i: (0, 0))
    return pl.pallas_call(
        _gla_gate_kernel,
        grid=(m // tm,),
        in_specs=[pl.BlockSpec((tm, k), lambda i: (i, 0)), full((k, r)), full((r, n)), full((1, n))],
        out_specs=pl.BlockSpec((tm, n), lambda i: (i, 0)),
        out_shape=jax.ShapeDtypeStruct((m, n), f32),
        compiler_params=_params("parallel"),
        name="gla_gate",
    )(u, w1, w2, bias)


def _pool_prompt_kernel(u_ref, w_ref, sc_ref, o_ref, *, batch):
    b = pl.program_id(0)
    g = pl.program_id(1)

    @pl.when(b == batch)
    def _():
        o_ref[...] = jnp.zeros_like(o_ref)

    x = u_ref[...]
    row = lax.broadcasted_iota(jnp.int32, x.shape, 0)
    pos = lax.broadcasted_iota(jnp.int32, (x.shape[0], 1), 0) + 1
    for gi, win in enumerate(POOL_WINDOWS):
        @pl.when((g == gi) & (b < batch))
        def _(win=win):
            s = x
            sh = 1
            while sh < win:
                s = s + jnp.where(row >= sh, pltpu.roll(s, sh, axis=0), 0.0)
                sh *= 2
            cnt = jnp.minimum(pos, win).astype(f32)
            d = s / cnt - x
            y = jnp.dot(d.astype(bf16), w_ref[...].astype(bf16), preferred_element_type=f32)
            o_ref[...] = y * sc_ref[...]


def _pool_prompt(u32, w, scale, j, *, batch, seq):
    groups, gc = w.shape[1], w.shape[2]
    assert 0 < u32.shape[0] - batch * seq <= seq
    return pl.pallas_call(
        functools.partial(_pool_prompt_kernel, batch=batch),
        grid=(batch + 1, groups),
        in_specs=[
            pl.BlockSpec((seq, gc), lambda b, g: (jnp.minimum(b, batch - 1), g)),
            pl.BlockSpec((None, None, gc, gc), lambda b, g: (j, g, 0, 0)),
            pl.BlockSpec((None, 1, gc), lambda b, g: (j, 0, g)),
        ],
        out_specs=pl.BlockSpec((seq, gc), lambda b, g: (b, g)),
        out_shape=jax.ShapeDtypeStruct((u32.shape[0], groups * gc), f32),
        compiler_params=_params("parallel", "parallel"),
        name="pool_prompt",
    )(u32, w, scale)


def _pool_sample_kernel(u_ref, buf_ref, w_ref, sc_ref, o_ref):
    g = pl.program_id(0)
    x = u_ref[...]
    nbuf = buf_ref.shape[1]
    for gi, win in enumerate(POOL_WINDOWS):
        @pl.when(g == gi)
        def _(win=win):
            s = x
            for r in range(nbuf - (win - 1), nbuf):
                s = s + buf_ref[:, r, :]
            d = s / float(win) - x
            y = jnp.dot(d.astype(bf16), w_ref[...].astype(bf16), preferred_element_type=f32)
            o_ref[...] = y * sc_ref[...]


def _pool_sample(u32, buf, w, scale, j, *, row0, batch):
    groups, gc = w.shape[1], w.shape[2]
    nbuf = buf.shape[2]
    return pl.pallas_call(
        _pool_sample_kernel,
        grid=(groups,),
        in_specs=[
            pl.BlockSpec((batch, gc), lambda g: (row0 // batch, g)),
            pl.BlockSpec((None, batch, nbuf, gc), lambda g: (j, 0, 0, g)),
            pl.BlockSpec((None, None, gc, gc), lambda g: (j, g, 0, 0)),
            pl.BlockSpec((None, 1, gc), lambda g: (j, 0, g)),
        ],
        out_specs=pl.BlockSpec((batch, gc), lambda g: (0, g)),
        out_shape=jax.ShapeDtypeStruct((batch, groups * gc), f32),
        compiler_params=_params("parallel"),
        name="pool_sample",
    )(u32, buf, w, scale)


def _xattn_prompt_kernel(q_ref, k_ref, v_ref, o_ref, *, heads, steps):
    step = pl.program_id(0)

    @pl.when(step == steps)
    def _():
        o_ref[...] = jnp.zeros_like(o_ref)

    @pl.when(step < steps)
    def _():
        dh = q_ref.shape[1] // heads
        scale = dh ** -0.5
        for h in range(heads):
            c = slice(h * dh, (h + 1) * dh)
            s = lax.dot_general(q_ref[:, c], k_ref[:, c].astype(bf16), (((1,), (1,)), ((), ())),
                                preferred_element_type=f32) * scale
            p = jnp.exp(s - jnp.max(s, axis=-1, keepdims=True))
            p = p / jnp.sum(p, axis=-1, keepdims=True)
            o = jnp.dot(p.astype(bf16), v_ref[:, c].astype(bf16), preferred_element_type=f32)
            o_ref[:, c] = o.astype(o_ref.dtype)


def _xattn_prompt(q, mem_k, mem_v, *, batch, seq, n_mem, tq):
    d = q.shape[1]
    nq = seq // tq
    steps = batch * nq
    assert nq * tq == seq and 0 < q.shape[0] - batch * seq <= tq
    tile = lambda s: jnp.minimum(s, steps - 1)
    kv = pl.BlockSpec((n_mem, d), lambda s: (tile(s) // nq, 0))
    return pl.pallas_call(
        functools.partial(_xattn_prompt_kernel, heads=XA_HEADS, steps=steps),
        grid=(steps + 1,),
        in_specs=[pl.BlockSpec((tq, d), lambda s: (tile(s), 0)), kv, kv],
        out_specs=pl.BlockSpec((tq, d), lambda s: (s, 0)),
        out_shape=jax.ShapeDtypeStruct((q.shape[0], d), bf16),
        compiler_params=_params("parallel"),
        name="xattn_prompt",
    )(q, mem_k, mem_v)


def _xattn_sample_kernel(q_ref, k_ref, v_ref, o_ref, *, bb):
    n_mem, heads, dh = k_ref.shape[1:]
    rows = n_mem * heads
    scale = dh ** -0.5
    qrows = q_ref.shape[1]
    col_head = lax.broadcasted_iota(jnp.int32, (qrows, rows), 1) & (heads - 1)
    row_head = lax.broadcasted_iota(jnp.int32, (qrows, rows), 0) & (heads - 1)
    own = col_head == row_head
    for r in range(bb):
        k2 = k_ref[r].reshape(rows, dh).astype(bf16)
        v2 = v_ref[r].reshape(rows, dh).astype(bf16)
        s = lax.dot_general(q_ref[r], k2, (((1,), (1,)), ((), ())), preferred_element_type=f32) * scale
        s = jnp.where(own, s, -jnp.inf)
        p = jnp.exp(s - jnp.max(s, axis=-1, keepdims=True))
        p = p / jnp.sum(p, axis=-1, keepdims=True)
        o_ref[r] = jnp.dot(p.astype(bf16), v2, preferred_element_type=f32)


def _xattn_sample(q, cache_k, cache_v, layer, *, bb):
    batch, qrows, dh = q.shape
    n_mem, heads = cache_k.shape[2], cache_k.shape[3]
    assert heads & (heads - 1) == 0 and qrows % heads == 0
    kv = pl.BlockSpec((None, bb, n_mem, heads, dh), lambda b: (layer, b, 0, 0, 0))
    qo = pl.BlockSpec((bb, qrows, dh), lambda b: (b, 0, 0))
    return pl.pallas_call(
        functools.partial(_xattn_sample_kernel, bb=bb),
        grid=(batch // bb,),
        in_specs=[qo, kv, kv],
        out_specs=qo,
        out_shape=jax.ShapeDtypeStruct((batch, qrows, dh), f32),
        compiler_params=_params("parallel"),
        name="xattn_sample",
    )(q, cache_k, cache_v)


def kernel(x_prompt, x_sample, state_hgrn, state_gla, state_pool, cache_mem_k, cache_mem_v, mem_prompt, norm_gains, hgrn_w_in, hgrn_lb, hgrn_g_norm, hgrn_w_o, gla_w_in, gla_w_gk1, gla_w_gk2, gla_b_gk, gla_g_norm, gla_w_o, pool_w, pool_scale, mem_norm, xa_w_q, xa_w_k, xa_w_v, xa_w_o, mlp_w_up, mlp_w_down):
    batch, seq, d = x_prompt.shape
    dec_batch = x_sample.shape[0]
    depth = norm_gains.shape[0]
    n_mem = mem_prompt.shape[1]
    hg_heads = state_hgrn.shape[2]
    gla_dk, gla_dv = state_gla.shape[3], state_gla.shape[4]
    n_prompt = batch * seq
    m_rows = n_prompt + dec_batch
    tm = m_rows // TOKEN_TILES_WS
    tr = m_rows // TOKEN_TILES_RES
    assert tm * TOKEN_TILES_WS == m_rows and tm % 16 == 0
    assert tr * TOKEN_TILES_RES == m_rows and tr % 16 == 0
    dh = d // XA_HEADS

    x = jnp.concatenate([x_prompt.reshape(n_prompt, d), x_sample.reshape(dec_batch, d)], axis=0)
    mem = mem_prompt.reshape(batch * n_mem, d)
    gain = lambda i, n: norm_gains[i, n][None, :]
    with_sample = lambda full, rows: lax.dynamic_update_slice(full, rows.astype(full.dtype), (n_prompt, 0))

    lbs = _lower_bounds(hgrn_lb)[:, None, :]
    hgrn_g_norm = hgrn_g_norm[:, None, :]
    gla_g_norm = gla_g_norm[:, None, :]
    pool_scale = pool_scale[:, None, :]
    hgrn_w_o_b = _cast_bf16(hgrn_w_o)
    gla_w_o_b = _cast_bf16(gla_w_o)
    xa_w_o_b = _cast_bf16(xa_w_o)
    mlp_w_down_b = _cast_bf16(mlp_w_down)
    rank = gla_w_gk1.shape[2]
    gk1 = jnp.pad(gla_w_gk1, ((0, 0), (0, 0), (0, 128 - rank)))
    gk2 = jnp.pad(gla_w_gk2, ((0, 0), (0, 128 - rank), (0, 0)))

    new_h_p, new_g_p, new_g_s, new_p_p, new_p_s, mem_ks, mem_vs = [], [], [], [], [], [], []
    new_h_s = None
    u = _rmsnorm(x, gain(0, 0), bf16, tr)
    for i in range(depth):
        j, kind = i // 3, i % 3
        if kind == 0:
            proj = _mm_ws(u, hgrn_w_in, j, tm=tm, tn=1024)
            o, h_p = _hgrn_prompt(proj, lbs, i, hgrn_g_norm, j, batch=batch, seq=seq, heads=hg_heads, lt=512)
            o_s, new_h_s = _hgrn_sample(proj, lbs, i, hgrn_g_norm, j, state_hgrn, new_h_s, row0=n_prompt,
                                        batch=dec_batch, heads=hg_heads, bb=8, hb=4)
            new_h_p.append(h_p)
            x, u = _mm_res(with_sample(o, o_s), hgrn_w_o_b, j, x, gain(i, 1), gain(i, 2), tm=tr, tk=d)
        elif kind == 1:
            proj = _mm_ws(u, gla_w_in, j, tm=tm, tn=1024)
            logg = _gla_gate(u, gk1[j], gk2[j], gla_b_gk[j][None, :], tm=tm)
            o, g_p = _gla_prompt(proj, logg, gla_g_norm, j, batch=batch, seq=seq, heads=GLA_HEADS,
                                 dk=gla_dk, dv=gla_dv, lt=256)
            o_s, g_s = _gla_sample(proj, logg, gla_g_norm, j, state_gla, row0=n_prompt, batch=dec_batch,
                                   heads=GLA_HEADS, dk=gla_dk, dv=gla_dv, bb=8)
            new_g_p.append(g_p)
            new_g_s.append(g_s)
            x, u = _mm_res(with_sample(o, o_s), gla_w_o_b, j, x, gain(i, 1), gain(i, 2), tm=tr, tk=d)
        else:
            u32 = _rmsnorm(x, gain(i, 0), f32, tr)
            m = _pool_prompt(u32, pool_w, pool_scale, j, batch=batch, seq=seq)
            m_s = _pool_sample(u32, state_pool, pool_w, pool_scale, j, row0=n_prompt, batch=dec_batch)
            nbuf = state_pool.shape[2]
            new_p_p.append(jnp.stack([u32[(b + 1) * seq - nbuf:(b + 1) * seq] for b in range(batch)]))
            new_p_s.append(jnp.concatenate([state_pool[j][:, 1:], u32[n_prompt:][:, None, :]], axis=1))
            x, u = _add_norm(x, with_sample(m, m_s), gain(i, 1), gain(i, 2), tr)

        q = _mm_ws(u, xa_w_q, i, tm=tm, tn=1024, out_dtype=bf16)
        mem_n = _rmsnorm(mem, mem_norm[i][None, :], bf16, 512)
        mem_k = _mm_ws(mem_n, xa_w_k, i, tm=512, tn=1024)
        mem_v = _mm_ws(mem_n, xa_w_v, i, tm=512, tn=1024)
        mem_ks.append(mem_k)
        mem_vs.append(mem_v)
        a = _xattn_prompt(q, mem_k, mem_v, batch=batch, seq=seq, n_mem=n_mem, tq=512)
        q_s = jnp.tile(q[n_prompt:].reshape(dec_batch, XA_HEADS, dh), (1, BF16_ROWS // XA_HEADS, 1))
        a_s = _xattn_sample(q_s, cache_mem_k, cache_mem_v, i, bb=4)[:, :XA_HEADS]
        x, u = _mm_res(with_sample(a, a_s.reshape(dec_batch, d)), xa_w_o_b, i, x, gain(i, 3), gain(i, 4),
                       tm=tr, tk=d)

        h = _mm_ws(u, mlp_w_up, i, tm=tm, tn=1024, out_dtype=bf16, act="relu2")
        g_next = gain(i + 1, 0) if i + 1 < depth else None
        x, u = _mm_res(h, mlp_w_down_b, i, x, gain(i, 5), g_next, tm=tr, tk=2048)

    kv_shape = (depth, batch, n_mem, XA_HEADS, dh)
    return (
        x[:n_prompt].reshape(batch, seq, d),
        x[n_prompt:].reshape(dec_batch, 1, d),
        jnp.stack(new_h_p),
        new_h_s,
        jnp.stack(new_g_p),
        jnp.stack(new_g_s),
        jnp.stack(new_p_p),
        jnp.stack(new_p_s),
        jnp.stack(mem_ks).reshape(kv_shape),
        jnp.stack(mem_vs).reshape(kv_shape),
    )
```

```python
import functools

import jax
import jax.numpy as jnp
from jax import lax
from jax.experimental import pallas as pl
from jax.experimental.pallas import tpu as pltpu

f32 = jnp.float32
bf16 = jnp.bfloat16

EPS = 1e-6
LOG2E = 1.4426950408889634
POOL_WINDOWS = (2, 4, 8, 16)
GLA_GATE_NORMALIZER = 16.0
XA_HEADS = 4
HG_DK = 128
GLA_HEADS = 4
CHUNK = 32
LANES = 128
SUBLANES = 8
BF16_ROWS = 16
TOKEN_TILES_WS = 5
TOKEN_TILES_RES = 13
VMEM_LIMIT = 56 * 1024 * 1024


def _params(*sem, vmem=VMEM_LIMIT):
    return pltpu.CompilerParams(dimension_semantics=sem, vmem_limit_bytes=vmem)


def _rms_scale(x):
    return lax.rsqrt(jnp.mean(x * x, axis=-1, keepdims=True) + EPS)


def _silu(x):
    return x / (1.0 + jnp.exp(-x))


def _rmsnorm_kernel(x_ref, g_ref, o_ref):
    x = x_ref[...]
    o_ref[...] = (x * _rms_scale(x) * g_ref[...]).astype(o_ref.dtype)


def _rmsnorm(x, g, out_dtype, tm):
    m, d = x.shape
    return pl.pallas_call(
        _rmsnorm_kernel,
        grid=(m // tm,),
        in_specs=[pl.BlockSpec((tm, d), lambda i: (i, 0)), pl.BlockSpec((1, d), lambda i: (0, 0))],
        out_specs=pl.BlockSpec((tm, d), lambda i: (i, 0)),
        out_shape=jax.ShapeDtypeStruct((m, d), out_dtype),
        compiler_params=_params("parallel"),
        name="rmsnorm",
    )(x, g)


def _add_norm_kernel(x_ref, m_ref, gp_ref, gn_ref, xo_ref, uo_ref):
    m = m_ref[...]
    xn = x_ref[...] + m * _rms_scale(m) * gp_ref[...]
    xo_ref[...] = xn
    uo_ref[...] = (xn * _rms_scale(xn) * gn_ref[...]).astype(uo_ref.dtype)


def _add_norm(x, m, g_post, g_next, tm):
    rows, d = x.shape
    row = pl.BlockSpec((tm, d), lambda i: (i, 0))
    gain = pl.BlockSpec((1, d), lambda i: (0, 0))
    return pl.pallas_call(
        _add_norm_kernel,
        grid=(rows // tm,),
        in_specs=[row, row, gain, gain],
        out_specs=[row, row],
        out_shape=[jax.ShapeDtypeStruct((rows, d), f32), jax.ShapeDtypeStruct((rows, d), bf16)],
        compiler_params=_params("parallel"),
        name="add_norm",
    )(x, m, g_post, g_next)


def _cast_kernel(x_ref, o_ref):
    o_ref[...] = x_ref[...].astype(o_ref.dtype)


def _cast_bf16(w, tr=512):
    nl, k, n = w.shape
    spec = pl.BlockSpec((None, tr, n), lambda l, i: (l, i, 0))
    return pl.pallas_call(
        _cast_kernel,
        grid=(nl, k // tr),
        in_specs=[spec],
        out_specs=spec,
        out_shape=jax.ShapeDtypeStruct(w.shape, bf16),
        compiler_params=_params("parallel", "parallel"),
        name="cast_bf16",
    )(w)


def _lower_bounds_kernel(l_ref, o_ref):
    z = l_ref[...]
    p = jnp.exp(z - jnp.max(z, axis=0, keepdims=True))
    p = p / jnp.sum(p, axis=0, keepdims=True)
    acc = jnp.zeros_like(p[0:1])
    for i in range(z.shape[0]):
        acc = acc + p[i:i + 1]
        o_ref[i:i + 1, :] = acc - p[0:1]


def _lower_bounds(lb_logits):
    return pl.pallas_call(
        _lower_bounds_kernel,
        out_shape=jax.ShapeDtypeStruct(lb_logits.shape, f32),
        name="hgrn_lower_bounds",
    )(lb_logits)


def _mm_ws_kernel(x_ref, w_ref, o_ref, wbf_ref, *, act):
    @pl.when(pl.program_id(1) == 0)
    def _():
        wbf_ref[...] = w_ref[...].astype(bf16)

    acc = jnp.dot(x_ref[...], wbf_ref[...], preferred_element_type=f32)
    if act == "relu2":
        acc = jnp.square(jnp.maximum(acc, 0.0))
    o_ref[...] = acc.astype(o_ref.dtype)


def _mm_ws(x, w, layer, *, tm, tn, out_dtype=f32, act=None):
    m, k = x.shape
    n = w.shape[2]
    return pl.pallas_call(
        functools.partial(_mm_ws_kernel, act=act),
        grid=(n // tn, m // tm),
        in_specs=[
            pl.BlockSpec((tm, k), lambda j, i: (i, 0)),
            pl.BlockSpec((None, k, tn), lambda j, i: (layer, 0, j)),
        ],
        out_specs=pl.BlockSpec((tm, tn), lambda j, i: (i, j)),
        out_shape=jax.ShapeDtypeStruct((m, n), out_dtype),
        scratch_shapes=[pltpu.VMEM((k, tn), bf16)],
        compiler_params=_params("arbitrary", "arbitrary"),
        name="matmul_ws",
    )(x, w)


def _mm_res_kernel(a_ref, w_ref, x_ref, gp_ref, *rest, nk, with_next):
    rest = list(rest)
    gn_ref = rest.pop(0) if with_next else None
    xo_ref = rest.pop(0)
    uo_ref = rest.pop(0) if with_next else None

    def finish(m, rows=slice(None)):
        xn = x_ref[rows, :] + m * _rms_scale(m) * gp_ref[...]
        xo_ref[rows, :] = xn
        if with_next:
            uo_ref[rows, :] = (xn * _rms_scale(xn) * gn_ref[...]).astype(uo_ref.dtype)

    if nk == 1:
        half = a_ref.shape[0] // 2
        for rows in (slice(0, half), slice(half, 2 * half)):
            finish(jnp.dot(a_ref[rows, :], w_ref[...], preferred_element_type=f32), rows)
        return
    acc_ref = rest.pop(0)
    kk = pl.program_id(1)

    @pl.when(kk == 0)
    def _():
        acc_ref[...] = jnp.zeros_like(acc_ref)

    acc_ref[...] += jnp.dot(a_ref[...], w_ref[...], preferred_element_type=f32)

    @pl.when(kk == nk - 1)
    def _():
        finish(acc_ref[...])


def _mm_res(a, w, layer, x, g_post, g_next, *, tm, tk):
    m, k = a.shape
    d = w.shape[2]
    nk = k // tk
    with_next = g_next is not None
    row = pl.BlockSpec((tm, d), lambda i, kk: (i, 0))
    gain = pl.BlockSpec((1, d), lambda i, kk: (0, 0))
    in_specs = [
        pl.BlockSpec((tm, tk), lambda i, kk: (i, kk)),
        pl.BlockSpec((None, tk, d), lambda i, kk: (layer, kk, 0)),
        row,
        gain,
    ]
    args = [a, w, x, g_post]
    out_specs = [row]
    out_shape = [jax.ShapeDtypeStruct((m, d), f32)]
    if with_next:
        in_specs.append(gain)
        args.append(g_next)
        out_specs.append(row)
        out_shape.append(jax.ShapeDtypeStruct((m, d), bf16))
    res = pl.pallas_call(
        functools.partial(_mm_res_kernel, nk=nk, with_next=with_next),
        grid=(m // tm, nk),
        in_specs=in_specs,
        out_specs=out_specs,
        out_shape=out_shape,
        scratch_shapes=[pltpu.VMEM((tm, d), f32)] if nk > 1 else [],
        compiler_params=_params("parallel", "arbitrary"),
        name="matmul_residual_norm",
    )(*args)
    return (res[0], res[1]) if with_next else (res[0], None)


def _segmented_cumsum(x, seg):
    pos = lax.broadcasted_iota(jnp.int32, x.shape, 0) & (seg - 1)
    sh = 1
    while sh < seg:
        x = x + jnp.where(pos >= sh, pltpu.roll(x, sh, axis=0), 0.0)
        sh *= 2
    return x


def _recurrence_tile(q, k, v, logg, st_ref, *, k_nonneg=False):
    r, dk = q.shape
    dv = v.shape[1]
    c, g8 = CHUNK, SUBLANES
    nc, ng = r // c, c // g8
    n8 = nc * ng
    b = _segmented_cumsum(logg * LOG2E, c)
    q3, k3, b3 = (x.reshape(nc, c, dk) for x in (q, k, b))
    v3 = v.astype(bf16).reshape(nc, c, dv)
    last = b3[:, c - 1:c, :]
    qe = (q3 * jnp.exp2(b3)).astype(bf16)
    ke = (k3 * jnp.exp2(last - b3)).astype(bf16)

    qg, kg, bg = (x.reshape(n8, g8, dk) for x in (q, k, b))
    if k_nonneg:
        cg = bg - jnp.log2(kg)
    lane = lax.broadcasted_iota(jnp.int32, (n8, g8, c), 2)
    group = lax.broadcasted_iota(jnp.int32, (n8, g8, c), 0) & (ng - 1)
    rel = lane - group * g8
    row = lax.broadcasted_iota(jnp.int32, (n8, g8, c), 1)
    diag = jnp.zeros((n8, g8, c), f32)
    for j in range(g8):
        if k_nonneg:
            col = jnp.sum(qg * jnp.exp2(bg - cg[:, j:j + 1, :]), axis=-1, keepdims=True)
        else:
            w = jnp.exp2(bg - bg[:, j:j + 1, :])
            col = jnp.sum(qg * w * kg[:, j:j + 1, :], axis=-1, keepdims=True)
        diag = jnp.where(rel == j, col, diag)
    diag = jnp.where(rel <= row, diag, 0.0)

    srow = lax.broadcasted_iota(jnp.int32, (nc, c, dk), 1)
    blocks = [jnp.zeros((nc, g8, c), f32)]
    for i in range(1, ng):
        lo = i * g8
        anchor = b3[:, lo:lo + 1, :]
        qi = (q3[:, lo:lo + g8, :] * jnp.exp2(b3[:, lo:lo + g8, :] - anchor)).astype(bf16)
        ki = jnp.where(srow < lo, k3 * jnp.exp2(anchor - b3), 0.0).astype(bf16)
        blocks.append(jnp.einsum("ctk,csk->cts", qi, ki, preferred_element_type=f32))
    a = jnp.concatenate(blocks, axis=1) + diag.reshape(nc, c, c)

    o_intra = jnp.einsum("cts,csv->ctv", a.astype(bf16), v3, preferred_element_type=f32)
    kvt = jnp.einsum("csv,csk->cvk", v3, ke, preferred_element_type=f32)
    st = st_ref[...]
    o_inter = []
    for ci in range(nc):
        o_inter.append(lax.dot_general(qe[ci], st.astype(bf16), (((1,), (1,)), ((), ())),
                                       preferred_element_type=f32))
        st = st * jnp.exp2(last[ci]) + kvt[ci]
    st_ref[...] = st
    return (o_intra + jnp.stack(o_inter)).reshape(r, dv)


def _gated_head_norm(o, gate, gn):
    return o * _rms_scale(o) * gn * _silu(gate)


def _hgrn_features(q_raw, fz, lb):
    e = jnp.exp(-jnp.abs(fz))
    r = 1.0 / (1.0 + e)
    nonneg = fz >= 0
    sig = jnp.where(nonneg, r, e * r)
    sig_neg = jnp.where(nonneg, e * r, r)
    log_sig = jnp.minimum(fz, 0.0) - jnp.log(1.0 + e)
    logf = jnp.where(lb > 0, jnp.log(lb + (1.0 - lb) * sig), log_sig)
    k = (1.0 - lb) * sig_neg
    return _silu(q_raw), k, logf


def _sequence_step(o_ref, so_ref, s_ref, *, nt, steps, compute):
    step = pl.program_id(1)
    t = lax.rem(step, nt)
    real = step < steps

    @pl.when(t == 0)
    def _():
        s_ref[...] = jnp.zeros_like(s_ref)

    o_ref[...] = jnp.where(real, compute(), 0.0).astype(o_ref.dtype)

    @pl.when((t == nt - 1) & real)
    def _():
        so_ref[...] = jnp.transpose(s_ref[...])


def _sequence_call(kernel, in_blocks, args, gains, *, name, rows, batch, seq, heads, dk, dv, lt):
    nt = seq // lt
    steps = batch * nt
    assert nt * lt == seq and 0 < rows - batch * seq <= lt
    tile = lambda s: jnp.minimum(s, steps - 1)
    tok = lambda width, off: pl.BlockSpec((lt, width), lambda h, s: (tile(s), off + h))

    def small(idx, width, per_head):
        return pl.BlockSpec((None, 1, width), lambda h, s: (idx, 0, h if per_head else 0))

    return pl.pallas_call(
        functools.partial(kernel, nt=nt, steps=steps),
        grid=(heads, steps + 1),
        in_specs=[tok(w, off) for w, off in in_blocks] + [small(i, w, p) for _, i, w, p in gains],
        out_specs=[
            pl.BlockSpec((lt, dv), lambda h, s: (s, h)),
            pl.BlockSpec((None, None, dk, dv), lambda h, s: (tile(s) // nt, h, 0, 0)),
        ],
        out_shape=[
            jax.ShapeDtypeStruct((rows, heads * dv), bf16),
            jax.ShapeDtypeStruct((batch, heads, dk, dv), f32),
        ],
        scratch_shapes=[pltpu.VMEM((dv, dk), f32)],
        compiler_params=_params("parallel", "arbitrary"),
        name=name,
    )(*args, *[g[0] for g in gains])


def _hgrn_prompt_kernel(q_ref, fz_ref, v_ref, g_ref, lb_ref, gn_ref, o_ref, so_ref, s_ref, *, nt, steps):
    def compute():
        q, k, logf = _hgrn_features(q_ref[...], fz_ref[...], lb_ref[...])
        o = _recurrence_tile(q, k, v_ref[...], logf, s_ref, k_nonneg=True)
        return _gated_head_norm(o, g_ref[...], gn_ref[...])

    _sequence_step(o_ref, so_ref, s_ref, nt=nt, steps=steps, compute=compute)


def _hgrn_prompt(proj, lbs, layer, g_norm, j, *, batch, seq, heads, lt):
    dk = HG_DK
    dv = g_norm.shape[-1]
    return _sequence_call(
        _hgrn_prompt_kernel, [(dk, 0), (dk, heads), (dk, 2 * heads), (dk, 3 * heads)], [proj] * 4,
        [(lbs, layer, dk, True), (g_norm, j, dv, False)],
        name="hgrn_prompt", rows=proj.shape[0], batch=batch, seq=seq, heads=heads, dk=dk, dv=dv, lt=lt)


def _gla_prompt_kernel(q_ref, k_ref, v_ref, g_ref, lg_ref, gn_ref, o_ref, so_ref, s_ref, *, nt, steps, q_scale):
    def compute():
        o = _recurrence_tile(q_ref[...] * q_scale, k_ref[...], v_ref[...], lg_ref[...], s_ref)
        return _gated_head_norm(o, g_ref[...], gn_ref[...])

    _sequence_step(o_ref, so_ref, s_ref, nt=nt, steps=steps, compute=compute)


def _gla_prompt(proj, logg, g_norm, j, *, batch, seq, heads, dk, dv, lt):
    v_off = 2 * heads * dk // dv
    return _sequence_call(
        functools.partial(_gla_prompt_kernel, q_scale=dk ** -0.5),
        [(dk, 0), (dk, heads), (dv, v_off), (dv, v_off + heads), (dk, 0)], [proj] * 4 + [logg],
        [(g_norm, j, dv, False)],
        name="gla_prompt", rows=proj.shape[0], batch=batch, seq=seq, heads=heads, dk=dk, dv=dv, lt=lt)


def _split3(x):
    hi = x.astype(bf16).astype(f32)
    rest = x - hi
    mid = rest.astype(bf16).astype(f32)
    lo = (rest - mid).astype(bf16).astype(f32)
    return hi, mid, lo


def _state_step(q, k, v, logg, s_ref, so_ref, hh):
    bb, dk = q.shape
    dv = v.shape[1]
    assert dv % LANES == 0 and dv & (dv - 1) == 0 and bb & (bb - 1) == 0

    def blocks(rows, width):
        shape = (rows, bb * width)
        return (lax.broadcasted_iota(jnp.int32, shape, 0),
                lax.broadcasted_iota(jnp.int32, shape, 1) >> (width.bit_length() - 1))

    row, blk = blocks(4 * bb, LANES)
    ones_blk = jnp.where((blk == (row & (bb - 1))) & (row < 3 * bb), 1.0, 0.0).astype(bf16)
    row, blk = blocks(2 * bb, dv)
    v_blk = jnp.where(blk == row, jnp.tile(jnp.concatenate([v, jnp.zeros_like(v)], axis=0), (1, bb)), 0.0)
    pad = jnp.zeros((bb, dk), f32)
    over_rows = (((0,), (0,)), ((), ()))
    kv = lax.dot_general(jnp.concatenate([k, pad], axis=0).astype(bf16), v_blk.astype(bf16), over_rows,
                         preferred_element_type=f32)
    e_all = lax.dot_general(jnp.concatenate([*_split3(jnp.exp(logg)), pad], axis=0).astype(bf16), ones_blk,
                            over_rows, preferred_element_type=f32)
    q_all = lax.dot_general(jnp.concatenate([*_split3(q), pad], axis=0).astype(bf16), ones_blk, over_rows,
                            preferred_element_type=f32)
    outs = []
    for r in range(bb):
        e_r = e_all[:, r * LANES:(r + 1) * LANES]
        q_r = q_all[:, r * LANES:(r + 1) * LANES]
        parts = []
        for c0 in range(0, dv, LANES):
            s_new = e_r * s_ref[r, hh, :, c0:c0 + LANES] + kv[:, r * dv + c0:r * dv + c0 + LANES]
            so_ref[r, hh, :, c0:c0 + LANES] = s_new
            parts.append(jnp.sum(q_r * s_new, axis=0, keepdims=True))
        outs.append(jnp.concatenate(parts, axis=1))
    return jnp.concatenate(outs, axis=0)


def _hgrn_sample_kernel(q_ref, fz_ref, v_ref, g_ref, lb_ref, gn_ref, s_ref, *rest, hb, j, first):
    o_ref, so_ref = rest[-2:]
    if first:
        for layer in range(so_ref.shape[0]):
            if layer != j:
                so_ref[layer] = jnp.zeros(so_ref.shape[1:], f32)
        so_ref = so_ref.at[j]
    dk = HG_DK
    dv = gn_ref.shape[1]
    gn = gn_ref[...]
    for hh in range(hb):
        kc = slice(hh * dk, (hh + 1) * dk)
        vc = slice(hh * dv, (hh + 1) * dv)
        q, k, logf = _hgrn_features(q_ref[:, kc], fz_ref[:, kc], lb_ref[:, kc])
        o = _state_step(q, k, v_ref[:, vc], logf, s_ref, so_ref, hh)
        o_ref[:, vc] = _gated_head_norm(o, g_ref[:, vc], gn)


def _hgrn_sample(proj, lbs, layer, g_norm, j, state, new_states, *, row0, batch, heads, bb, hb):
    dk = HG_DK
    dv = g_norm.shape[-1]
    rb0 = row0 // bb
    hblocks = heads // hb
    n_layers = state.shape[0]
    first = new_states is None
    col = lambda off: pl.BlockSpec((bb, hb * dk), lambda b, h: (rb0 + b, off + h))
    st = pl.BlockSpec((None, bb, hb, dk, dv), lambda b, h: (j, b, h, 0, 0))
    in_specs = [
        col(0), col(hblocks), col(2 * hblocks), col(3 * hblocks),
        pl.BlockSpec((None, 1, hb * dk), lambda b, h: (layer, 0, h)),
        pl.BlockSpec((None, 1, dv), lambda b, h: (j, 0, 0)),
        st,
    ]
    args = [proj, proj, proj, proj, lbs, g_norm, state]
    if first:
        st_out = pl.BlockSpec((n_layers, bb, hb, dk, dv), lambda b, h: (0, b, h, 0, 0))
        aliases = {}
    else:
        st_out = st
        in_specs.append(pl.BlockSpec(memory_space=pl.ANY))
        args.append(new_states)
        aliases = {len(args) - 1: 1}
    return pl.pallas_call(
        functools.partial(_hgrn_sample_kernel, hb=hb, j=j, first=first),
        grid=(batch // bb, hblocks),
        in_specs=in_specs,
        out_specs=[pl.BlockSpec((bb, hb * dv), lambda b, h: (b, h)), st_out],
        out_shape=[
            jax.ShapeDtypeStruct((batch, heads * dv), f32),
            jax.ShapeDtypeStruct(state.shape, f32),
        ],
        input_output_aliases=aliases,
        compiler_params=_params("parallel", "parallel"),
        name="hgrn_sample",
    )(*args)


def _gla_sample_kernel(q_ref, k_ref, v_ref, g_ref, lg_ref, gn_ref, s_ref, o_ref, so_ref, *, q_scale):
    o = _state_step(q_ref[...] * q_scale, k_ref[...], v_ref[...], lg_ref[...], s_ref, so_ref, 0)
    o_ref[...] = _gated_head_norm(o, g_ref[...], gn_ref[...])


def _gla_sample(proj, logg, g_norm, j, state, *, row0, batch, heads, dk, dv, bb):
    rb0 = row0 // bb
    kcol = lambda off: pl.BlockSpec((bb, dk), lambda b, h: (rb0 + b, off + h))
    vcol = lambda off: pl.BlockSpec((bb, dv), lambda b, h: (rb0 + b, off + h))
    v_off = 2 * heads * dk // dv
    return pl.pallas_call(
        functools.partial(_gla_sample_kernel, q_scale=dk ** -0.5),
        grid=(batch // bb, heads),
        in_specs=[
            kcol(0), kcol(heads), vcol(v_off), vcol(v_off + heads), kcol(0),
            pl.BlockSpec((None, 1, dv), lambda b, h: (j, 0, 0)),
            pl.BlockSpec((None, bb, 1, dk, dv), lambda b, h: (j, b, h, 0, 0)),
        ],
        out_specs=[
            pl.BlockSpec((bb, dv), lambda b, h: (b, h)),
            pl.BlockSpec((bb, 1, dk, dv), lambda b, h: (b, h, 0, 0)),
        ],
        out_shape=[
            jax.ShapeDtypeStruct((batch, heads * dv), f32),
            jax.ShapeDtypeStruct((batch, heads, dk, dv), f32),
        ],
        compiler_params=_params("parallel", "parallel"),
        name="gla_sample",
    )(proj, proj, proj, proj, logg, g_norm, state)


def _gla_gate_kernel(u_ref, w1_ref, w2_ref, b_ref, o_ref):
    low = jnp.dot(u_ref[...], w1_ref[...].astype(bf16), preferred_element_type=f32)
    gk = jnp.dot(low.astype(bf16), w2_ref[...].astype(bf16), preferred_element_type=f32) + b_ref[...]
    log_sig = jnp.minimum(gk, 0.0) - jnp.log1p(jnp.exp(-jnp.abs(gk)))
    o_ref[...] = log_sig / GLA_GATE_NORMALIZER


def _gla_gate(u, w1, w2, bias, *, tm):
    m, k = u.shape
    r = w1.shape[1]
    n = w2.shape[1]
    full = lambda shape: pl.BlockSpec(shape, lambda i: (0, 0))
    return pl.pallas_call(
        _gla_gate_kernel,
        grid=(m // tm,),
        in_specs=[pl.BlockSpec((tm, k), lambda i: (i, 0)), full((k, r)), full((r, n)), full((1, n))],
        out_specs=pl.BlockSpec((tm, n), lambda i: (i, 0)),
        out_shape=jax.ShapeDtypeStruct((m, n), f32),
        compiler_params=_params("parallel"),
        name="gla_gate",
    )(u, w1, w2, bias)


def _pool_prompt_kernel(u_ref, w_ref, sc_ref, o_ref, *, batch):
    b = pl.program_id(0)
    g = pl.program_id(1)

    @pl.when(b == batch)
    def _():
        o_ref[...] = jnp.zeros_like(o_ref)

    x = u_ref[...]
    row = lax.broadcasted_iota(jnp.int32, x.shape, 0)
    pos = lax.broadcasted_iota(jnp.int32, (x.shape[0], 1), 0) + 1
    for gi, win in enumerate(POOL_WINDOWS):
        @pl.when((g == gi) & (b < batch))
        def _(win=win):
            s = x
            sh = 1
            while sh < win:
                s = s + jnp.where(row >= sh, pltpu.roll(s, sh, axis=0), 0.0)
                sh *= 2
            cnt = jnp.minimum(pos, win).astype(f32)
            d = s / cnt - x
            y = jnp.dot(d.astype(bf16), w_ref[...].astype(bf16), preferred_element_type=f32)
            o_ref[...] = y * sc_ref[...]


def _pool_prompt(u32, w, scale, j, *, batch, seq):
    groups, gc = w.shape[1], w.shape[2]
    assert 0 < u32.shape[0] - batch * seq <= seq
    return pl.pallas_call(
        functools.partial(_pool_prompt_kernel, batch=batch),
        grid=(batch + 1, groups),
        in_specs=[
            pl.BlockSpec((seq, gc), lambda b, g: (jnp.minimum(b, batch - 1), g)),
            pl.BlockSpec((None, None, gc, gc), lambda b, g: (j, g, 0, 0)),
            pl.BlockSpec((None, 1, gc), lambda b, g: (j, 0, g)),
        ],
        out_specs=pl.BlockSpec((seq, gc), lambda b, g: (b, g)),
        out_shape=jax.ShapeDtypeStruct((u32.shape[0], groups * gc), f32),
        compiler_params=_params("parallel", "parallel"),
        name="pool_prompt",
    )(u32, w, scale)


def _pool_sample_kernel(u_ref, buf_ref, w_ref, sc_ref, o_ref):
    g = pl.program_id(0)
    x = u_ref[...]
    nbuf = buf_ref.shape[1]
    for gi, win in enumerate(POOL_WINDOWS):
        @pl.when(g == gi)
        def _(win=win):
            s = x
            for r in range(nbuf - (win - 1), nbuf):
                s = s + buf_ref[:, r, :]
            d = s / float(win) - x
            y = jnp.dot(d.astype(bf16), w_ref[...].astype(bf16), preferred_element_type=f32)
            o_ref[...] = y * sc_ref[...]


def _pool_sample(u32, buf, w, scale, j, *, row0, batch):
    groups, gc = w.shape[1], w.shape[2]
    nbuf = buf.shape[2]
    return pl.pallas_call(
        _pool_sample_kernel,
        grid=(groups,),
        in_specs=[
            pl.BlockSpec((batch, gc), lambda g: (row0 // batch, g)),
            pl.BlockSpec((None, batch, nbuf, gc), lambda g: (j, 0, 0, g)),
            pl.BlockSpec((None, None, gc, gc), lambda g: (j, g, 0, 0)),
            pl.BlockSpec((None, 1, gc), lambda g: (j, 0, g)),
        ],
        out_specs=pl.BlockSpec((batch, gc), lambda g: (0, g)),
        out_shape=jax.ShapeDtypeStruct((batch, groups * gc), f32),
        compiler_params=_params("parallel"),
        name="pool_sample",
    )(u32, buf, w, scale)


def _xattn_prompt_kernel(q_ref, k_ref, v_ref, o_ref, *, heads, steps):
    real = pl.program_id(0) < steps
    dh = q_ref.shape[1] // heads
    scale = dh ** -0.5
    for h in range(heads):
        c = slice(h * dh, (h + 1) * dh)
        s = lax.dot_general(q_ref[:, c], k_ref[:, c].astype(bf16), (((1,), (1,)), ((), ())),
                            preferred_element_type=f32) * scale
        p = jnp.exp(s - jnp.max(s, axis=-1, keepdims=True))
        p = p / jnp.sum(p, axis=-1, keepdims=True)
        o = jnp.dot(p.astype(bf16), v_ref[:, c].astype(bf16), preferred_element_type=f32)
        o_ref[:, c] = jnp.where(real, o, 0.0).astype(o_ref.dtype)


def _xattn_prompt(q, mem_k, mem_v, *, batch, seq, n_mem, tq):
    d = q.shape[1]
    nq = seq // tq
    steps = batch * nq
    assert nq * tq == seq and 0 < q.shape[0] - batch * seq <= tq
    tile = lambda s: jnp.minimum(s, steps - 1)
    kv = pl.BlockSpec((n_mem, d), lambda s: (tile(s) // nq, 0))
    return pl.pallas_call(
        functools.partial(_xattn_prompt_kernel, heads=XA_HEADS, steps=steps),
        grid=(steps + 1,),
        in_specs=[pl.BlockSpec((tq, d), lambda s: (tile(s), 0)), kv, kv],
        out_specs=pl.BlockSpec((tq, d), lambda s: (s, 0)),
        out_shape=jax.ShapeDtypeStruct((q.shape[0], d), bf16),
        compiler_params=_params("parallel"),
        name="xattn_prompt",
    )(q, mem_k, mem_v)


def _xattn_sample_kernel(q_ref, k_ref, v_ref, o_ref, *, bb):
    n_mem, heads, dh = k_ref.shape[1:]
    rows = n_mem * heads
    scale = dh ** -0.5
    qrows = q_ref.shape[1]
    col_head = lax.broadcasted_iota(jnp.int32, (qrows, rows), 1) & (heads - 1)
    row_head = lax.broadcasted_iota(jnp.int32, (qrows, rows), 0) & (heads - 1)
    own = col_head == row_head
    for r in range(bb):
        k2 = k_ref[r].reshape(rows, dh).astype(bf16)
        v2 = v_ref[r].reshape(rows, dh).astype(bf16)
        s = lax.dot_general(q_ref[r], k2, (((1,), (1,)), ((), ())), preferred_element_type=f32) * scale
        s = jnp.where(own, s, -jnp.inf)
        p = jnp.exp(s - jnp.max(s, axis=-1, keepdims=True))
        p = p / jnp.sum(p, axis=-1, keepdims=True)
        o_ref[r] = jnp.dot(p.astype(bf16), v2, preferred_element_type=f32)


def _xattn_sample(q, cache_k, cache_v, layer, *, bb):
    batch, qrows, dh = q.shape
    n_mem, heads = cache_k.shape[2], cache_k.shape[3]
    assert heads & (heads - 1) == 0 and qrows % heads == 0
    kv = pl.BlockSpec((None, bb, n_mem, heads, dh), lambda b: (layer, b, 0, 0, 0))
    qo = pl.BlockSpec((bb, qrows, dh), lambda b: (b, 0, 0))
    return pl.pallas_call(
        functools.partial(_xattn_sample_kernel, bb=bb),
        grid=(batch // bb,),
        in_specs=[qo, kv, kv],
        out_specs=qo,
        out_shape=jax.ShapeDtypeStruct((batch, qrows, dh), f32),
        compiler_params=_params("parallel"),
        name="xattn_sample",
    )(q, cache_k, cache_v)


def kernel(x_prompt, x_sample, state_hgrn, state_gla, state_pool, cache_mem_k, cache_mem_v, mem_prompt, norm_gains, hgrn_w_in, hgrn_lb, hgrn_g_norm, hgrn_w_o, gla_w_in, gla_w_gk1, gla_w_gk2, gla_b_gk, gla_g_norm, gla_w_o, pool_w, pool_scale, mem_norm, xa_w_q, xa_w_k, xa_w_v, xa_w_o, mlp_w_up, mlp_w_down):
    batch, seq, d = x_prompt.shape
    dec_batch = x_sample.shape[0]
    depth = norm_gains.shape[0]
    n_mem = mem_prompt.shape[1]
    hg_heads = state_hgrn.shape[2]
    gla_dk, gla_dv = state_gla.shape[3], state_gla.shape[4]
    n_prompt = batch * seq
    m_rows = n_prompt + dec_batch
    tm = m_rows // TOKEN_TILES_WS
    tr = m_rows // TOKEN_TILES_RES
    assert tm * TOKEN_TILES_WS == m_rows and tm % 16 == 0
    assert tr * TOKEN_TILES_RES == m_rows and tr % 16 == 0
    dh = d // XA_HEADS

    x = jnp.concatenate([x_prompt.reshape(n_prompt, d), x_sample.reshape(dec_batch, d)], axis=0)
    mem = mem_prompt.reshape(batch * n_mem, d)
    gain = lambda i, n: norm_gains[i, n][None, :]
    with_sample = lambda full, rows: lax.dynamic_update_slice(full, rows.astype(full.dtype), (n_prompt, 0))

    lbs = _lower_bounds(hgrn_lb)[:, None, :]
    hgrn_g_norm = hgrn_g_norm[:, None, :]
    gla_g_norm = gla_g_norm[:, None, :]
    pool_scale = pool_scale[:, None, :]
    hgrn_w_o_b = _cast_bf16(hgrn_w_o)
    gla_w_o_b = _cast_bf16(gla_w_o)
    xa_w_o_b = _cast_bf16(xa_w_o)
    mlp_w_down_b = _cast_bf16(mlp_w_down)
    rank = gla_w_gk1.shape[2]
    gk1 = jnp.pad(gla_w_gk1, ((0, 0), (0, 0), (0, 128 - rank)))
    gk2 = jnp.pad(gla_w_gk2, ((0, 0), (0, 128 - rank), (0, 0)))

    new_h_p, new_g_p, new_g_s, new_p_p, new_p_s, mem_ks, mem_vs = [], [], [], [], [], [], []
    new_h_s = None
    u = _rmsnorm(x, gain(0, 0), bf16, tr)
    for i in range(depth):
        j, kind = i // 3, i % 3
        if kind == 0:
            proj = _mm_ws(u, hgrn_w_in, j, tm=tm, tn=1024)
            o, h_p = _hgrn_prompt(proj, lbs, i, hgrn_g_norm, j, batch=batch, seq=seq, heads=hg_heads, lt=512)
            o_s, new_h_s = _hgrn_sample(proj, lbs, i, hgrn_g_norm, j, state_hgrn, new_h_s, row0=n_prompt,
                                        batch=dec_batch, heads=hg_heads, bb=8, hb=4)
            new_h_p.append(h_p)
            x, u = _mm_res(with_sample(o, o_s), hgrn_w_o_b, j, x, gain(i, 1), gain(i, 2), tm=tr, tk=d)
        elif kind == 1:
            proj = _mm_ws(u, gla_w_in, j, tm=tm, tn=1024)
            logg = _gla_gate(u, gk1[j], gk2[j], gla_b_gk[j][None, :], tm=tm)
            o, g_p = _gla_prompt(proj, logg, gla_g_norm, j, batch=batch, seq=seq, heads=GLA_HEADS,
                                 dk=gla_dk, dv=gla_dv, lt=256)
            o_s, g_s = _gla_sample(proj, logg, gla_g_norm, j, state_gla, row0=n_prompt, batch=dec_batch,
                                   heads=GLA_HEADS, dk=gla_dk, dv=gla_dv, bb=8)
            new_g_p.append(g_p)
            new_g_s.append(g_s)
            x, u = _mm_res(with_sample(o, o_s), gla_w_o_b, j, x, gain(i, 1), gain(i, 2), tm=tr, tk=d)
        else:
            u32 = _rmsnorm(x, gain(i, 0), f32, tr)
            m = _pool_prompt(u32, pool_w, pool_scale, j, batch=batch, seq=seq)
            m_s = _pool_sample(u32, state_pool, pool_w, pool_scale, j, row0=n_prompt, batch=dec_batch)
            nbuf = state_pool.shape[2]
            new_p_p.append(jnp.stack([u32[(b + 1) * seq - nbuf:(b + 1) * seq] for b in range(batch)]))
            new_p_s.append(jnp.concatenate([state_pool[j][:, 1:], u32[n_prompt:][:, None, :]], axis=1))
            x, u = _add_norm(x, with_sample(m, m_s), gain(i, 1), gain(i, 2), tr)

        q = _mm_ws(u, xa_w_q, i, tm=tm, tn=1024, out_dtype=bf16)
        mem_n = _rmsnorm(mem, mem_norm[i][None, :], bf16, 512)
        mem_k = _mm_ws(mem_n, xa_w_k, i, tm=512, tn=1024)
        mem_v = _mm_ws(mem_n, xa_w_v, i, tm=512, tn=1024)
        mem_ks.append(mem_k)
        mem_vs.append(mem_v)
        a = _xattn_prompt(q, mem_k, mem_v, batch=batch, seq=seq, n_mem=n_mem, tq=512)
        q_s = jnp.tile(q[n_prompt:].reshape(dec_batch, XA_HEADS, dh), (1, BF16_ROWS // XA_HEADS, 1))
        a_s = _xattn_sample(q_s, cache_mem_k, cache_mem_v, i, bb=4)[:, :XA_HEADS]
        x, u = _mm_res(with_sample(a, a_s.reshape(dec_batch, d)), xa_w_o_b, i, x, gain(i, 3), gain(i, 4),
                       tm=tr, tk=d)

        h = _mm_ws(u, mlp_w_up, i, tm=tm, tn=1024, out_dtype=bf16, act="relu2")
        g_next = gain(i + 1, 0) if i + 1 < depth else None
        x, u = _mm_res(h, mlp_w_down_b, i, x, gain(i, 5), g_next, tm=tr, tk=2048)

    kv_shape = (depth, batch, n_mem, XA_HEADS, dh)
    return (
        x[:n_prompt].reshape(batch, seq, d),
        x[n_prompt:].reshape(dec_batch, 1, d),
        jnp.stack(new_h_p),
        new_h_s,
        jnp.stack(new_g_p),
        jnp.stack(new_g_s),
        jnp.stack(new_p_p),
        jnp.stack(new_p_s),
        jnp.stack(mem_ks).reshape(kv_shape),
        jnp.stack(mem_vs).reshape(kv_shape),
    )
```

```python
import functools

import jax
import jax.numpy as jnp
from jax import lax
from jax.experimental import pallas as pl
from jax.experimental.pallas import tpu as pltpu

f32 = jnp.float32
bf16 = jnp.bfloat16

EPS = 1e-6
LOG2E = 1.4426950408889634
POOL_WINDOWS = (2, 4, 8, 16)
GLA_GATE_NORMALIZER = 16.0
XA_HEADS = 4
HG_DK = 128
GLA_HEADS = 4
CHUNK = 32
LANES = 128
SUBLANES = 8
BF16_ROWS = 16
TOKEN_TILES_WS = 5
TOKEN_TILES_RES = 13
VMEM_LIMIT = 56 * 1024 * 1024


def _params(*sem, vmem=VMEM_LIMIT):
    return pltpu.CompilerParams(dimension_semantics=sem, vmem_limit_bytes=vmem)


def _rms_scale(x):
    return lax.rsqrt(jnp.mean(x * x, axis=-1, keepdims=True) + EPS)


def _silu(x):
    return x / (1.0 + jnp.exp(-x))


def _rmsnorm_kernel(x_ref, g_ref, o_ref):
    x = x_ref[...]
    o_ref[...] = (x * _rms_scale(x) * g_ref[...]).astype(o_ref.dtype)


def _rmsnorm(x, g, out_dtype, tm):
    m, d = x.shape
    return pl.pallas_call(
        _rmsnorm_kernel,
        grid=(m // tm,),
        in_specs=[pl.BlockSpec((tm, d), lambda i: (i, 0)), pl.BlockSpec((1, d), lambda i: (0, 0))],
        out_specs=pl.BlockSpec((tm, d), lambda i: (i, 0)),
        out_shape=jax.ShapeDtypeStruct((m, d), out_dtype),
        compiler_params=_params("parallel"),
        name="rmsnorm",
    )(x, g)


def _add_norm_kernel(x_ref, m_ref, gp_ref, gn_ref, xo_ref, uo_ref):
    m = m_ref[...]
    xn = x_ref[...] + m * _rms_scale(m) * gp_ref[...]
    xo_ref[...] = xn
    uo_ref[...] = (xn * _rms_scale(xn) * gn_ref[...]).astype(uo_ref.dtype)


def _add_norm(x, m, g_post, g_next, tm):
    rows, d = x.shape
    row = pl.BlockSpec((tm, d), lambda i: (i, 0))
    gain = pl.BlockSpec((1, d), lambda i: (0, 0))
    return pl.pallas_call(
        _add_norm_kernel,
        grid=(rows // tm,),
        in_specs=[row, row, gain, gain],
        out_specs=[row, row],
        out_shape=[jax.ShapeDtypeStruct((rows, d), f32), jax.ShapeDtypeStruct((rows, d), bf16)],
        compiler_params=_params("parallel"),
        name="add_norm",
    )(x, m, g_post, g_next)


def _cast_kernel(x_ref, o_ref):
    o_ref[...] = x_ref[...].astype(o_ref.dtype)


def _cast_bf16(w, tr=512):
    nl, k, n = w.shape
    spec = pl.BlockSpec((None, tr, n), lambda l, i: (l, i, 0))
    return pl.pallas_call(
        _cast_kernel,
        grid=(nl, k // tr),
        in_specs=[spec],
        out_specs=spec,
        out_shape=jax.ShapeDtypeStruct(w.shape, bf16),
        compiler_params=_params("parallel", "parallel"),
        name="cast_bf16",
    )(w)


def _lower_bounds_kernel(l_ref, o_ref):
    z = l_ref[...]
    p = jnp.exp(z - jnp.max(z, axis=0, keepdims=True))
    p = p / jnp.sum(p, axis=0, keepdims=True)
    acc = jnp.zeros_like(p[0:1])
    for i in range(z.shape[0]):
        acc = acc + p[i:i + 1]
        o_ref[i:i + 1, :] = acc - p[0:1]


def _lower_bounds(lb_logits):
    return pl.pallas_call(
        _lower_bounds_kernel,
        out_shape=jax.ShapeDtypeStruct(lb_logits.shape, f32),
        name="hgrn_lower_bounds",
    )(lb_logits)


def _mm_ws_kernel(x_ref, w_ref, o_ref, wbf_ref, *, act):
    @pl.when(pl.program_id(1) == 0)
    def _():
        wbf_ref[...] = w_ref[...].astype(bf16)

    acc = jnp.dot(x_ref[...], wbf_ref[...], preferred_element_type=f32)
    if act == "relu2":
        acc = jnp.square(jnp.maximum(acc, 0.0))
    o_ref[...] = acc.astype(o_ref.dtype)


def _mm_ws(x, w, layer, *, tm, tn, out_dtype=f32, act=None):
    m, k = x.shape
    n = w.shape[2]
    return pl.pallas_call(
        functools.partial(_mm_ws_kernel, act=act),
        grid=(n // tn, m // tm),
        in_specs=[
            pl.BlockSpec((tm, k), lambda j, i: (i, 0)),
            pl.BlockSpec((None, k, tn), lambda j, i: (layer, 0, j)),
        ],
        out_specs=pl.BlockSpec((tm, tn), lambda j, i: (i, j)),
        out_shape=jax.ShapeDtypeStruct((m, n), out_dtype),
        scratch_shapes=[pltpu.VMEM((k, tn), bf16)],
        compiler_params=_params("arbitrary", "arbitrary"),
        name="matmul_ws",
    )(x, w)


def _mm_res_kernel(a_ref, w_ref, x_ref, gp_ref, *rest, nk, with_next, tail):
    rest = list(rest)
    gn_ref = rest.pop(0) if with_next else None
    xo_ref = rest.pop(0)
    uo_ref = rest.pop(0) if with_next else None
    xt_ref = rest.pop(0) if tail else None

    def finish(m, rows=slice(None)):
        xn = x_ref[rows, :] + m * _rms_scale(m) * gp_ref[...]
        xo_ref[rows, :] = xn
        if with_next:
            uo_ref[rows, :] = (xn * _rms_scale(xn) * gn_ref[...]).astype(uo_ref.dtype)
        if tail:
            @pl.when(pl.program_id(0) == pl.num_programs(0) - 1)
            def _():
                xt_ref[...] = xn[tail[0]:tail[0] + tail[1], :]

    if nk == 1:
        half = a_ref.shape[0] // 2
        for rows in (slice(0, half), slice(half, 2 * half)):
            finish(jnp.dot(a_ref[rows, :], w_ref[...], preferred_element_type=f32), rows)
        return
    acc_ref = rest.pop(0)
    kk = pl.program_id(1)

    @pl.when(kk == 0)
    def _():
        acc_ref[...] = jnp.zeros_like(acc_ref)

    acc_ref[...] += jnp.dot(a_ref[...], w_ref[...], preferred_element_type=f32)

    @pl.when(kk == nk - 1)
    def _():
        finish(acc_ref[...])


def _mm_res(a, w, layer, x, g_post, g_next, *, tm, tk, split_rows=None):
    m, k = a.shape
    d = w.shape[2]
    nk = k // tk
    with_next = g_next is not None
    tail = None
    if split_rows is not None:
        last = (m // tm - 1) * tm
        assert not with_next and nk > 1 and last < split_rows < m
        tail = (split_rows - last, m - split_rows)
    row = pl.BlockSpec((tm, d), lambda i, kk: (i, 0))
    gain = pl.BlockSpec((1, d), lambda i, kk: (0, 0))
    in_specs = [
        pl.BlockSpec((tm, tk), lambda i, kk: (i, kk)),
        pl.BlockSpec((None, tk, d), lambda i, kk: (layer, kk, 0)),
        row,
        gain,
    ]
    args = [a, w, x, g_post]
    out_specs = [row]
    out_shape = [jax.ShapeDtypeStruct((m, d), f32)]
    if with_next:
        in_specs.append(gain)
        args.append(g_next)
        out_specs.append(row)
        out_shape.append(jax.ShapeDtypeStruct((m, d), bf16))
    if tail:
        out_shape[0] = jax.ShapeDtypeStruct((split_rows, d), f32)
        out_specs.append(pl.BlockSpec((tail[1], d), lambda i, kk: (0, 0)))
        out_shape.append(jax.ShapeDtypeStruct((tail[1], d), f32))
    res = pl.pallas_call(
        functools.partial(_mm_res_kernel, nk=nk, with_next=with_next, tail=tail),
        grid=(m // tm, nk),
        in_specs=in_specs,
        out_specs=out_specs,
        out_shape=out_shape,
        scratch_shapes=[pltpu.VMEM((tm, d), f32)] if nk > 1 else [],
        compiler_params=_params("arbitrary" if tail else "parallel", "arbitrary"),
        name="matmul_residual_norm",
    )(*args)
    return (res[0], res[1]) if with_next or tail else (res[0], None)


def _segmented_cumsum(x, seg):
    pos = lax.broadcasted_iota(jnp.int32, x.shape, 0) & (seg - 1)
    sh = 1
    while sh < seg:
        x = x + jnp.where(pos >= sh, pltpu.roll(x, sh, axis=0), 0.0)
        sh *= 2
    return x


def _recurrence_tile(q, k, v, logg, st_ref):
    r, dk = q.shape
    dv = v.shape[1]
    c, g8 = CHUNK, SUBLANES
    nc, ng = r // c, c // g8
    n8 = nc * ng
    b = _segmented_cumsum(logg * LOG2E, c)
    q3, k3, b3 = (x.reshape(nc, c, dk) for x in (q, k, b))
    v3 = v.astype(bf16).reshape(nc, c, dv)
    last = b3[:, c - 1:c, :]
    qe = (q3 * jnp.exp2(b3)).astype(bf16)
    ke = (k3 * jnp.exp2(last - b3)).astype(bf16)

    qg, kg, bg = (x.reshape(n8, g8, dk) for x in (q, k, b))
    lane = lax.broadcasted_iota(jnp.int32, (n8, g8, c), 2)
    group = lax.broadcasted_iota(jnp.int32, (n8, g8, c), 0) & (ng - 1)
    rel = lane - group * g8
    row = lax.broadcasted_iota(jnp.int32, (n8, g8, c), 1)
    diag = jnp.zeros((n8, g8, c), f32)
    for j in range(g8):
        w = jnp.exp2(bg - bg[:, j:j + 1, :])
        col = jnp.sum(qg * w * kg[:, j:j + 1, :], axis=-1, keepdims=True)
        diag = jnp.where(rel == j, col, diag)
    diag = jnp.where(rel <= row, diag, 0.0)

    srow = lax.broadcasted_iota(jnp.int32, (nc, c, dk), 1)
    blocks = [jnp.zeros((nc, g8, c), f32)]
    for i in range(1, ng):
        lo = i * g8
        anchor = b3[:, lo:lo + 1, :]
        qi = (q3[:, lo:lo + g8, :] * jnp.exp2(b3[:, lo:lo + g8, :] - anchor)).astype(bf16)
        ki = jnp.where(srow < lo, k3 * jnp.exp2(anchor - b3), 0.0).astype(bf16)
        blocks.append(jnp.einsum("ctk,csk->cts", qi, ki, preferred_element_type=f32))
    a = jnp.concatenate(blocks, axis=1) + diag.reshape(nc, c, c)

    o_intra = jnp.einsum("cts,csv->ctv", a.astype(bf16), v3, preferred_element_type=f32)
    kvt = jnp.einsum("csv,csk->cvk", v3, ke, preferred_element_type=f32)
    st = st_ref[...]
    o_inter = []
    for ci in range(nc):
        o_inter.append(lax.dot_general(qe[ci], st.astype(bf16), (((1,), (1,)), ((), ())),
                                       preferred_element_type=f32))
        st = st * jnp.exp2(last[ci]) + kvt[ci]
    st_ref[...] = st
    return (o_intra + jnp.stack(o_inter)).reshape(r, dv)


def _gated_head_norm(o, gate, gn):
    return o * _rms_scale(o) * gn * _silu(gate)


def _hgrn_features(q_raw, fz, lb):
    e = jnp.exp(-jnp.abs(fz))
    r = 1.0 / (1.0 + e)
    nonneg = fz >= 0
    sig = jnp.where(nonneg, r, e * r)
    sig_neg = jnp.where(nonneg, e * r, r)
    log_sig = jnp.minimum(fz, 0.0) - jnp.log(1.0 + e)
    logf = jnp.where(lb > 0, jnp.log(lb + (1.0 - lb) * sig), log_sig)
    k = (1.0 - lb) * sig_neg
    return _silu(q_raw), k, logf


def _sequence_step(o_ref, so_ref, s_ref, *, nt, steps, compute):
    step = pl.program_id(1)
    t = lax.rem(step, nt)

    @pl.when(step == steps)
    def _():
        o_ref[...] = jnp.zeros_like(o_ref)

    @pl.when(step < steps)
    def _():
        @pl.when(t == 0)
        def _():
            s_ref[...] = jnp.zeros_like(s_ref)

        o_ref[...] = compute().astype(o_ref.dtype)

        @pl.when(t == nt - 1)
        def _():
            so_ref[...] = jnp.transpose(s_ref[...])


def _sequence_call(kernel, in_blocks, args, gains, *, name, rows, batch, seq, heads, dk, dv, lt):
    nt = seq // lt
    steps = batch * nt
    assert nt * lt == seq and 0 < rows - batch * seq <= lt
    tile = lambda s: jnp.minimum(s, steps - 1)
    tok = lambda width, off: pl.BlockSpec((lt, width), lambda h, s: (tile(s), off + h))

    def small(idx, width, per_head):
        return pl.BlockSpec((None, 1, width), lambda h, s: (idx, 0, h if per_head else 0))

    return pl.pallas_call(
        functools.partial(kernel, nt=nt, steps=steps),
        grid=(heads, steps + 1),
        in_specs=[tok(w, off) for w, off in in_blocks] + [small(i, w, p) for _, i, w, p in gains],
        out_specs=[
            pl.BlockSpec((lt, dv), lambda h, s: (s, h)),
            pl.BlockSpec((None, None, dk, dv), lambda h, s: (tile(s) // nt, h, 0, 0)),
        ],
        out_shape=[
            jax.ShapeDtypeStruct((rows, heads * dv), bf16),
            jax.ShapeDtypeStruct((batch, heads, dk, dv), f32),
        ],
        scratch_shapes=[pltpu.VMEM((dv, dk), f32)],
        compiler_params=_params("parallel", "arbitrary"),
        name=name,
    )(*args, *[g[0] for g in gains])


def _hgrn_prompt_kernel(q_ref, fz_ref, v_ref, g_ref, lb_ref, gn_ref, o_ref, so_ref, s_ref, *, nt, steps):
    def compute():
        q, k, logf = _hgrn_features(q_ref[...], fz_ref[...], lb_ref[...])
        o = _recurrence_tile(q, k, v_ref[...], logf, s_ref)
        return _gated_head_norm(o, g_ref[...], gn_ref[...])

    _sequence_step(o_ref, so_ref, s_ref, nt=nt, steps=steps, compute=compute)


def _hgrn_prompt(proj, lbs, layer, g_norm, j, *, batch, seq, heads, lt):
    dk = HG_DK
    dv = g_norm.shape[-1]
    return _sequence_call(
        _hgrn_prompt_kernel, [(dk, 0), (dk, heads), (dk, 2 * heads), (dk, 3 * heads)], [proj] * 4,
        [(lbs, layer, dk, True), (g_norm, j, dv, False)],
        name="hgrn_prompt", rows=proj.shape[0], batch=batch, seq=seq, heads=heads, dk=dk, dv=dv, lt=lt)


def _gla_prompt_kernel(q_ref, k_ref, v_ref, g_ref, lg_ref, gn_ref, o_ref, so_ref, s_ref, *, nt, steps, q_scale):
    def compute():
        o = _recurrence_tile(q_ref[...] * q_scale, k_ref[...], v_ref[...], lg_ref[...], s_ref)
        return _gated_head_norm(o, g_ref[...], gn_ref[...])

    _sequence_step(o_ref, so_ref, s_ref, nt=nt, steps=steps, compute=compute)


def _gla_prompt(proj, logg, g_norm, j, *, batch, seq, heads, dk, dv, lt):
    v_off = 2 * heads * dk // dv
    return _sequence_call(
        functools.partial(_gla_prompt_kernel, q_scale=dk ** -0.5),
        [(dk, 0), (dk, heads), (dv, v_off), (dv, v_off + heads), (dk, 0)], [proj] * 4 + [logg],
        [(g_norm, j, dv, False)],
        name="gla_prompt", rows=proj.shape[0], batch=batch, seq=seq, heads=heads, dk=dk, dv=dv, lt=lt)


def _split3(x):
    hi = x.astype(bf16).astype(f32)
    rest = x - hi
    mid = rest.astype(bf16).astype(f32)
    lo = (rest - mid).astype(bf16).astype(f32)
    return hi, mid, lo


def _state_step(q, k, v, logg, s_ref, so_ref, hh):
    bb, dk = q.shape
    dv = v.shape[1]
    assert dv % LANES == 0 and dv & (dv - 1) == 0 and bb & (bb - 1) == 0

    def blocks(rows, width):
        shape = (rows, bb * width)
        return (lax.broadcasted_iota(jnp.int32, shape, 0),
                lax.broadcasted_iota(jnp.int32, shape, 1) >> (width.bit_length() - 1))

    row, blk = blocks(4 * bb, LANES)
    ones_blk = jnp.where((blk == (row & (bb - 1))) & (row < 3 * bb), 1.0, 0.0).astype(bf16)
    row, blk = blocks(2 * bb, dv)
    v_blk = jnp.where(blk == row, jnp.tile(jnp.concatenate([v, jnp.zeros_like(v)], axis=0), (1, bb)), 0.0)
    pad = jnp.zeros((bb, dk), f32)
    over_rows = (((0,), (0,)), ((), ()))
    kv = lax.dot_general(jnp.concatenate([k, pad], axis=0).astype(bf16), v_blk.astype(bf16), over_rows,
                         preferred_element_type=f32)
    e_all = lax.dot_general(jnp.concatenate([*_split3(jnp.exp(logg)), pad], axis=0).astype(bf16), ones_blk,
                            over_rows, preferred_element_type=f32)
    q_all = lax.dot_general(jnp.concatenate([*_split3(q), pad], axis=0).astype(bf16), ones_blk, over_rows,
                            preferred_element_type=f32)
    outs = []
    for r in range(bb):
        e_r = e_all[:, r * LANES:(r + 1) * LANES]
        q_r = q_all[:, r * LANES:(r + 1) * LANES]
        parts = []
        for c0 in range(0, dv, LANES):
            s_new = e_r * s_ref[r, hh, :, c0:c0 + LANES] + kv[:, r * dv + c0:r * dv + c0 + LANES]
            so_ref[r, hh, :, c0:c0 + LANES] = s_new
            parts.append(jnp.sum(q_r * s_new, axis=0, keepdims=True))
        outs.append(jnp.concatenate(parts, axis=1))
    return jnp.concatenate(outs, axis=0)


def _hgrn_sample_kernel(q_ref, fz_ref, v_ref, g_ref, lb_ref, gn_ref, s_ref, *rest, hb, j, first):
    o_ref, so_ref = rest[-2:]
    if first:
        for layer in range(so_ref.shape[0]):
            if layer != j:
                so_ref[layer] = jnp.zeros(so_ref.shape[1:], f32)
        so_ref = so_ref.at[j]
    dk = HG_DK
    dv = gn_ref.shape[1]
    gn = gn_ref[...]
    for hh in range(hb):
        kc = slice(hh * dk, (hh + 1) * dk)
        vc = slice(hh * dv, (hh + 1) * dv)
        q, k, logf = _hgrn_features(q_ref[:, kc], fz_ref[:, kc], lb_ref[:, kc])
        o = _state_step(q, k, v_ref[:, vc], logf, s_ref, so_ref, hh)
        o_ref[:, vc] = _gated_head_norm(o, g_ref[:, vc], gn)


def _hgrn_sample(proj, lbs, layer, g_norm, j, state, new_states, *, row0, batch, heads, bb, hb):
    dk = HG_DK
    dv = g_norm.shape[-1]
    rb0 = row0 // bb
    hblocks = heads // hb
    n_layers = state.shape[0]
    first = new_states is None
    col = lambda off: pl.BlockSpec((bb, hb * dk), lambda b, h: (rb0 + b, off + h))
    st = pl.BlockSpec((None, bb, hb, dk, dv), lambda b, h: (j, b, h, 0, 0))
    in_specs = [
        col(0), col(hblocks), col(2 * hblocks), col(3 * hblocks),
        pl.BlockSpec((None, 1, hb * dk), lambda b, h: (layer, 0, h)),
        pl.BlockSpec((None, 1, dv), lambda b, h: (j, 0, 0)),
        st,
    ]
    args = [proj, proj, proj, proj, lbs, g_norm, state]
    if first:
        st_out = pl.BlockSpec((n_layers, bb, hb, dk, dv), lambda b, h: (0, b, h, 0, 0))
        aliases = {}
    else:
        st_out = st
        in_specs.append(pl.BlockSpec(memory_space=pl.ANY))
        args.append(new_states)
        aliases = {len(args) - 1: 1}
    return pl.pallas_call(
        functools.partial(_hgrn_sample_kernel, hb=hb, j=j, first=first),
        grid=(batch // bb, hblocks),
        in_specs=in_specs,
        out_specs=[pl.BlockSpec((bb, hb * dv), lambda b, h: (b, h)), st_out],
        out_shape=[
            jax.ShapeDtypeStruct((batch, heads * dv), f32),
            jax.ShapeDtypeStruct(state.shape, f32),
        ],
        input_output_aliases=aliases,
        compiler_params=_params("parallel", "parallel"),
        name="hgrn_sample",
    )(*args)


def _gla_sample_kernel(q_ref, k_ref, v_ref, g_ref, lg_ref, gn_ref, s_ref, o_ref, so_ref, *, q_scale):
    o = _state_step(q_ref[...] * q_scale, k_ref[...], v_ref[...], lg_ref[...], s_ref, so_ref, 0)
    o_ref[...] = _gated_head_norm(o, g_ref[...], gn_ref[...])


def _gla_sample(proj, logg, g_norm, j, state, *, row0, batch, heads, dk, dv, bb):
    rb0 = row0 // bb
    kcol = lambda off: pl.BlockSpec((bb, dk), lambda b, h: (rb0 + b, off + h))
    vcol = lambda off: pl.BlockSpec((bb, dv), lambda b, h: (rb0 + b, off + h))
    v_off = 2 * heads * dk // dv
    return pl.pallas_call(
        functools.partial(_gla_sample_kernel, q_scale=dk ** -0.5),
        grid=(batch // bb, heads),
        in_specs=[
            kcol(0), kcol(heads), vcol(v_off), vcol(v_off + heads), kcol(0),
            pl.BlockSpec((None, 1, dv), lambda b, h: (j, 0, 0)),
            pl.BlockSpec((None, bb, 1, dk, dv), lambda b, h: (j, b, h, 0, 0)),
        ],
        out_specs=[
            pl.BlockSpec((bb, dv), lambda b, h: (b, h)),
            pl.BlockSpec((bb, 1, dk, dv), lambda b, h: (b, h, 0, 0)),
        ],
        out_shape=[
            jax.ShapeDtypeStruct((batch, heads * dv), f32),
            jax.ShapeDtypeStruct((batch, heads, dk, dv), f32),
        ],
        compiler_params=_params("parallel", "parallel"),
        name="gla_sample",
    )(proj, proj, proj, proj, logg, g_norm, state)


def _gla_gate_kernel(u_ref, w1_ref, w2_ref, b_ref, o_ref):
    low = jnp.dot(u_ref[...], w1_ref[...].astype(bf16), preferred_element_type=f32)
    gk = jnp.dot(low.astype(bf16), w2_ref[...].astype(bf16), preferred_element_type=f32) + b_ref[...]
    log_sig = jnp.minimum(gk, 0.0) - jnp.log1p(jnp.exp(-jnp.abs(gk)))
    o_ref[...] = log_sig / GLA_GATE_NORMALIZER


def _gla_gate(u, w1, w2, bias, *, tm):
    m, k = u.shape
    r = w1.shape[1]
    n = w2.shape[1]
    full = lambda shape: pl.BlockSpec(shape, lambda i: (0, 0))
    return pl.pallas_call(
        _gla_gate_kernel,
        grid=(m // tm,),
        in_specs=[pl.BlockSpec((tm, k), lambda i: (i, 0)), full((k, r)), full((r, n)), full((1, n))],
        out_specs=pl.BlockSpec((tm, n), lambda i: (i, 0)),
        out_shape=jax.ShapeDtypeStruct((m, n), f32),
        compiler_params=_params("parallel"),
        name="gla_gate",
    )(u, w1, w2, bias)


def _pool_prompt_kernel(u_ref, w_ref, sc_ref, o_ref, *, batch):
    b = pl.program_id(0)
    g = pl.program_id(1)

    @pl.when(b == batch)
    def _():
        o_ref[...] = jnp.zeros_like(o_ref)

    x = u_ref[...]
    row = lax.broadcasted_iota(jnp.int32, x.shape, 0)
    pos = lax.broadcasted_iota(jnp.int32, (x.shape[0], 1), 0) + 1
    for gi, win in enumerate(POOL_WINDOWS):
        @pl.when((g == gi) & (b < batch))
        def _(win=win):
            s = x
            sh = 1
            while sh < win:
                s = s + jnp.where(row >= sh, pltpu.roll(s, sh, axis=0), 0.0)
                sh *= 2
            cnt = jnp.minimum(pos, win).astype(f32)
            d = s / cnt - x
            y = jnp.dot(d.astype(bf16), w_ref[...].astype(bf16), preferred_element_type=f32)
            o_ref[...] = y * sc_ref[...]


def _pool_prompt(u32, w, scale, j, *, batch, seq):
    groups, gc = w.shape[1], w.shape[2]
    assert 0 < u32.shape[0] - batch * seq <= seq
    return pl.pallas_call(
        functools.partial(_pool_prompt_kernel, batch=batch),
        grid=(batch + 1, groups),
        in_specs=[
            pl.BlockSpec((seq, gc), lambda b, g: (jnp.minimum(b, batch - 1), g)),
            pl.BlockSpec((None, None, gc, gc), lambda b, g: (j, g, 0, 0)),
            pl.BlockSpec((None, 1, gc), lambda b, g: (j, 0, g)),
        ],
        out_specs=pl.BlockSpec((seq, gc), lambda b, g: (b, g)),
        out_shape=jax.ShapeDtypeStruct((u32.shape[0], groups * gc), f32),
        compiler_params=_params("parallel", "parallel"),
        name="pool_prompt",
    )(u32, w, scale)


def _pool_sample_kernel(u_ref, buf_ref, w_ref, sc_ref, o_ref):
    g = pl.program_id(0)
    x = u_ref[...]
    nbuf = buf_ref.shape[1]
    for gi, win in enumerate(POOL_WINDOWS):
        @pl.when(g == gi)
        def _(win=win):
            s = x
            for r in range(nbuf - (win - 1), nbuf):
                s = s + buf_ref[:, r, :]
            d = s / float(win) - x
            y = jnp.dot(d.astype(bf16), w_ref[...].astype(bf16), preferred_element_type=f32)
            o_ref[...] = y * sc_ref[...]


def _pool_sample(u32, buf, w, scale, j, *, row0, batch):
    groups, gc = w.shape[1], w.shape[2]
    nbuf = buf.shape[2]
    return pl.pallas_call(
        _pool_sample_kernel,
        grid=(groups,),
        in_specs=[
            pl.BlockSpec((batch, gc), lambda g: (row0 // batch, g)),
            pl.BlockSpec((None, batch, nbuf, gc), lambda g: (j, 0, 0, g)),
            pl.BlockSpec((None, None, gc, gc), lambda g: (j, g, 0, 0)),
            pl.BlockSpec((None, 1, gc), lambda g: (j, 0, g)),
        ],
        out_specs=pl.BlockSpec((batch, gc), lambda g: (0, g)),
        out_shape=jax.ShapeDtypeStruct((batch, groups * gc), f32),
        compiler_params=_params("parallel"),
        name="pool_sample",
    )(u32, buf, w, scale)


def _xattn_prompt_kernel(q_ref, k_ref, v_ref, o_ref, *, heads, steps):
    step = pl.program_id(0)

    @pl.when(step == steps)
    def _():
        o_ref[...] = jnp.zeros_like(o_ref)

    @pl.when(step < steps)
    def _():
        dh = q_ref.shape[1] // heads
        scale = dh ** -0.5
        for h in range(heads):
            c = slice(h * dh, (h + 1) * dh)
            s = lax.dot_general(q_ref[:, c], k_ref[:, c].astype(bf16), (((1,), (1,)), ((), ())),
                                preferred_element_type=f32) * scale
            p = jnp.exp(s - jnp.max(s, axis=-1, keepdims=True))
            p = p / jnp.sum(p, axis=-1, keepdims=True)
            o = jnp.dot(p.astype(bf16), v_ref[:, c].astype(bf16), preferred_element_type=f32)
            o_ref[:, c] = o.astype(o_ref.dtype)


def _xattn_prompt(q, mem_k, mem_v, *, batch, seq, n_mem, tq):
    d = q.shape[1]
    nq = seq // tq
    steps = batch * nq
    assert nq * tq == seq and 0 < q.shape[0] - batch * seq <= tq
    tile = lambda s: jnp.minimum(s, steps - 1)
    kv = pl.BlockSpec((n_mem, d), lambda s: (tile(s) // nq, 0))
    return pl.pallas_call(
        functools.partial(_xattn_prompt_kernel, heads=XA_HEADS, steps=steps),
        grid=(steps + 1,),
        in_specs=[pl.BlockSpec((tq, d), lambda s: (tile(s), 0)), kv, kv],
        out_specs=pl.BlockSpec((tq, d), lambda s: (s, 0)),
        out_shape=jax.ShapeDtypeStruct((q.shape[0], d), bf16),
        compiler_params=_params("parallel"),
        name="xattn_prompt",
    )(q, mem_k, mem_v)


def _xattn_sample_kernel(q_ref, k_ref, v_ref, o_ref, *, bb):
    n_mem, heads, dh = k_ref.shape[1:]
    rows = n_mem * heads
    scale = dh ** -0.5
    qrows = q_ref.shape[1]
    col_head = lax.broadcasted_iota(jnp.int32, (qrows, rows), 1) & (heads - 1)
    row_head = lax.broadcasted_iota(jnp.int32, (qrows, rows), 0) & (heads - 1)
    own = col_head == row_head
    for r in range(bb):
        k2 = k_ref[r].reshape(rows, dh).astype(bf16)
        v2 = v_ref[r].reshape(rows, dh).astype(bf16)
        s = lax.dot_general(q_ref[r], k2, (((1,), (1,)), ((), ())), preferred_element_type=f32) * scale
        s = jnp.where(own, s, -jnp.inf)
        p = jnp.exp(s - jnp.max(s, axis=-1, keepdims=True))
        p = p / jnp.sum(p, axis=-1, keepdims=True)
        o_ref[r] = jnp.dot(p.astype(bf16), v2, preferred_element_type=f32)


def _xattn_sample(q, cache_k, cache_v, layer, *, bb):
    batch, qrows, dh = q.shape
    n_mem, heads = cache_k.shape[2], cache_k.shape[3]
    assert heads & (heads - 1) == 0 and qrows % heads == 0
    kv = pl.BlockSpec((None, bb, n_mem, heads, dh), lambda b: (layer, b, 0, 0, 0))
    qo = pl.BlockSpec((bb, qrows, dh), lambda b: (b, 0, 0))
    return pl.pallas_call(
        functools.partial(_xattn_sample_kernel, bb=bb),
        grid=(batch // bb,),
        in_specs=[qo, kv, kv],
        out_specs=qo,
        out_shape=jax.ShapeDtypeStruct((batch, qrows, dh), f32),
        compiler_params=_params("parallel"),
        name="xattn_sample",
    )(q, cache_k, cache_v)


def kernel(x_prompt, x_sample, state_hgrn, state_gla, state_pool, cache_mem_k, cache_mem_v, mem_prompt, norm_gains, hgrn_w_in, hgrn_lb, hgrn_g_norm, hgrn_w_o, gla_w_in, gla_w_gk1, gla_w_gk2, gla_b_gk, gla_g_norm, gla_w_o, pool_w, pool_scale, mem_norm, xa_w_q, xa_w_k, xa_w_v, xa_w_o, mlp_w_up, mlp_w_down):
    batch, seq, d = x_prompt.shape
    dec_batch = x_sample.shape[0]
    depth = norm_gains.shape[0]
    n_mem = mem_prompt.shape[1]
    hg_heads = state_hgrn.shape[2]
    gla_dk, gla_dv = state_gla.shape[3], state_gla.shape[4]
    n_prompt = batch * seq
    m_rows = n_prompt + dec_batch
    tm = m_rows // TOKEN_TILES_WS
    tr = m_rows // TOKEN_TILES_RES
    assert tm * TOKEN_TILES_WS == m_rows and tm % 16 == 0
    assert tr * TOKEN_TILES_RES == m_rows and tr % 16 == 0
    dh = d // XA_HEADS

    x = jnp.concatenate([x_prompt.reshape(n_prompt, d), x_sample.reshape(dec_batch, d)], axis=0)
    mem = mem_prompt.reshape(batch * n_mem, d)
    gain = lambda i, n: norm_gains[i, n][None, :]
    with_sample = lambda full, rows: lax.dynamic_update_slice(full, rows.astype(full.dtype), (n_prompt, 0))

    lbs = _lower_bounds(hgrn_lb)[:, None, :]
    hgrn_g_norm = hgrn_g_norm[:, None, :]
    gla_g_norm = gla_g_norm[:, None, :]
    pool_scale = pool_scale[:, None, :]
    hgrn_w_o_b = _cast_bf16(hgrn_w_o)
    gla_w_o_b = _cast_bf16(gla_w_o)
    xa_w_o_b = _cast_bf16(xa_w_o)
    mlp_w_down_b = _cast_bf16(mlp_w_down)
    rank = gla_w_gk1.shape[2]
    gk1 = jnp.pad(gla_w_gk1, ((0, 0), (0, 0), (0, 128 - rank)))
    gk2 = jnp.pad(gla_w_gk2, ((0, 0), (0, 128 - rank), (0, 0)))

    new_h_p, new_g_p, new_g_s, new_p_p, new_p_s, mem_ks, mem_vs = [], [], [], [], [], [], []
    new_h_s = None
    u = _rmsnorm(x, gain(0, 0), bf16, tr)
    for i in range(depth):
        j, kind = i // 3, i % 3
        if kind == 0:
            proj = _mm_ws(u, hgrn_w_in, j, tm=tm, tn=1024)
            o, h_p = _hgrn_prompt(proj, lbs, i, hgrn_g_norm, j, batch=batch, seq=seq, heads=hg_heads, lt=512)
            o_s, new_h_s = _hgrn_sample(proj, lbs, i, hgrn_g_norm, j, state_hgrn, new_h_s, row0=n_prompt,
                                        batch=dec_batch, heads=hg_heads, bb=8, hb=8)
            new_h_p.append(h_p)
            x, u = _mm_res(with_sample(o, o_s), hgrn_w_o_b, j, x, gain(i, 1), gain(i, 2), tm=tr, tk=d)
        elif kind == 1:
            proj = _mm_ws(u, gla_w_in, j, tm=tm, tn=1024)
            logg = _gla_gate(u, gk1[j], gk2[j], gla_b_gk[j][None, :], tm=tm)
            o, g_p = _gla_prompt(proj, logg, gla_g_norm, j, batch=batch, seq=seq, heads=GLA_HEADS,
                                 dk=gla_dk, dv=gla_dv, lt=256)
            o_s, g_s = _gla_sample(proj, logg, gla_g_norm, j, state_gla, row0=n_prompt, batch=dec_batch,
                                   heads=GLA_HEADS, dk=gla_dk, dv=gla_dv, bb=8)
            new_g_p.append(g_p)
            new_g_s.append(g_s)
            x, u = _mm_res(with_sample(o, o_s), gla_w_o_b, j, x, gain(i, 1), gain(i, 2), tm=tr, tk=d)
        else:
            u32 = _rmsnorm(x, gain(i, 0), f32, tr)
            m = _pool_prompt(u32, pool_w, pool_scale, j, batch=batch, seq=seq)
            m_s = _pool_sample(u32, state_pool, pool_w, pool_scale, j, row0=n_prompt, batch=dec_batch)
            nbuf = state_pool.shape[2]
            new_p_p.append(jnp.stack([u32[(b + 1) * seq - nbuf:(b + 1) * seq] for b in range(batch)]))
            new_p_s.append(jnp.concatenate([state_pool[j][:, 1:], u32[n_prompt:][:, None, :]], axis=1))
            x, u = _add_norm(x, with_sample(m, m_s), gain(i, 1), gain(i, 2), tr)

        q = _mm_ws(u, xa_w_q, i, tm=tm, tn=1024, out_dtype=bf16)
        mem_n = _rmsnorm(mem, mem_norm[i][None, :], bf16, 512)
        mem_k = _mm_ws(mem_n, xa_w_k, i, tm=512, tn=1024)
        mem_v = _mm_ws(mem_n, xa_w_v, i, tm=512, tn=1024)
        mem_ks.append(mem_k)
        mem_vs.append(mem_v)
        a = _xattn_prompt(q, mem_k, mem_v, batch=batch, seq=seq, n_mem=n_mem, tq=512)
        q_s = jnp.tile(q[n_prompt:].reshape(dec_batch, XA_HEADS, dh), (1, BF16_ROWS // XA_HEADS, 1))
        a_s = _xattn_sample(q_s, cache_mem_k, cache_mem_v, i, bb=4)[:, :XA_HEADS]
        x, u = _mm_res(with_sample(a, a_s.reshape(dec_batch, d)), xa_w_o_b, i, x, gain(i, 3), gain(i, 4),
                       tm=tr, tk=d)

        h = _mm_ws(u, mlp_w_up, i, tm=tm, tn=1024, out_dtype=bf16, act="relu2")
        if i + 1 < depth:
            x, u = _mm_res(h, mlp_w_down_b, i, x, gain(i, 5), gain(i + 1, 0), tm=tr, tk=2048)
        else:
            y_prompt, y_sample = _mm_res(h, mlp_w_down_b, i, x, gain(i, 5), None, tm=tr, tk=2048,
                                         split_rows=n_prompt)

    kv_shape = (depth, batch, n_mem, XA_HEADS, dh)
    return (
        y_prompt.reshape(batch, seq, d),
        y_sample.reshape(dec_batch, 1, d),
        jnp.stack(new_h_p),
        new_h_s,
        jnp.stack(new_g_p),
        jnp.stack(new_g_s),
        jnp.stack(new_p_p),
        jnp.stack(new_p_s),
        jnp.stack(mem_ks).reshape(kv_shape),
        jnp.stack(mem_vs).reshape(kv_shape),
    )
```
